```python
import jax, jax.numpy as jnp
from jax import lax
import numpy as np

D_MODEL = 4096
BATCH = 4
SEQ = 4096
DEPTH = 1

D_MIX = D_MODEL
RET_HEADS = 8
RET_QK_DIM = 128
RET_V_DIM = 256
MLSTM_HEADS = 4
MLSTM_QK_DIM = 256
MLSTM_V_DIM = 512
CONV_WIDTH = 4
D_FF = 11008
CHUNK = 128
ROPE_BASE = 10000.0
NORM_EPS = 1e-6
FFN_RES_WEIGHT = 0.5

RET_QK_W = RET_HEADS * RET_QK_DIM
RET_V_W = RET_HEADS * RET_V_DIM
MLSTM_QK_W = MLSTM_HEADS * MLSTM_QK_DIM
MLSTM_V_W = MLSTM_HEADS * MLSTM_V_DIM
D_IN_PROJ = 2 * RET_QK_W + 2 * RET_V_W + 2 * MLSTM_QK_W + 2 * MLSTM_V_W + 2 * MLSTM_HEADS

kernel_name = 'hybrid_retention_mlstm_macaron_layer'


def _rmsnorm(x, g):
    xf = x.astype(jnp.float32)
    y = xf * lax.rsqrt(jnp.mean(xf * xf, axis=-1, keepdims=True) + NORM_EPS)
    return (y * g.astype(jnp.float32)).astype(x.dtype)


def _head_rmsnorm(h, g):
    B, S, H, d = h.shape
    return _rmsnorm(h, g.reshape(H, d)).reshape(B, S, H * d)


def _swiglu(h, w_gate, w_up, w_down):
    a = jnp.einsum('bsd,df->bsf', h, w_gate)
    u = jnp.einsum('bsd,df->bsf', h, w_up)
    return jnp.einsum('bsf,fd->bsd', jax.nn.silu(a) * u, w_down)


def _rotary(t):
    S, d = t.shape[1], t.shape[-1]
    half = d // 2
    inv_freq = ROPE_BASE ** (-jnp.arange(half, dtype=jnp.float32) / half)
    ang = jnp.arange(S, dtype=jnp.float32)[:, None] * inv_freq[None, :]
    cos = jnp.cos(ang)[None, :, None, :]
    sin = jnp.sin(ang)[None, :, None, :]
    t1 = t[..., :half].astype(jnp.float32)
    t2 = t[..., half:].astype(jnp.float32)
    return jnp.concatenate([t1 * cos - t2 * sin, t1 * sin + t2 * cos], axis=-1).astype(t.dtype)


def _causal_conv(u, w, b):
    K = w.shape[0]
    S = u.shape[1]
    up = jnp.pad(u, ((0, 0), (K - 1, 0), (0, 0)))
    out = b
    for j in range(K):
        out = out + up[:, j:j + S, :] * w[j]
    return out


def _to_chunks(t):
    B, H, S = t.shape[:3]
    t = t.reshape((B, H, S // CHUNK, CHUNK) + t.shape[3:])
    return jnp.moveaxis(t, 2, 0)


def _from_chunks(t):
    t = jnp.moveaxis(t, 0, 2)
    B, H, nc, C = t.shape[:4]
    return t.reshape((B, H, nc * C) + t.shape[4:])


def _retention_chunkwise(q, k, v):
    B, H, S, dk = q.shape
    dv = v.shape[-1]
    log_gamma = jnp.log1p(-jnp.exp2(-5.0 - jnp.arange(H, dtype=jnp.float32)))
    pos = jnp.arange(CHUNK, dtype=jnp.float32)
    rel = pos[:, None] - pos[None, :]
    intra_decay = jnp.exp(jnp.where(rel >= 0, log_gamma[:, None, None] * rel, -jnp.inf))
    query_decay = jnp.exp(log_gamma[:, None] * (pos + 1.0))[:, :, None]
    key_decay = jnp.exp(log_gamma[:, None] * (CHUNK - 1.0 - pos))[:, :, None]
    chunk_decay = jnp.exp(log_gamma * CHUNK)[:, None, None]

    def step(state, qkv):
        qc, kc, vc = qkv
        scores = jnp.einsum('bhid,bhjd->bhij', qc, kc) * intra_decay
        out = (jnp.einsum('bhij,bhjv->bhiv', scores, vc)
               + jnp.einsum('bhid,bhdv->bhiv', qc, state) * query_decay)
        state = state * chunk_decay + jnp.einsum('bhjd,bhjv->bhdv', kc * key_decay, vc)
        return state, out

    state0 = jnp.zeros((B, H, dk, dv), jnp.float32)
    _, out = lax.scan(step, state0, (_to_chunks(q), _to_chunks(k), _to_chunks(v)))
    return _from_chunks(out).astype(v.dtype)


def _mlstm_chunkwise(q, k, v, i_pre, log_f):
    B, H, S, dk = q.shape
    dv = v.shape[-1]
    causal = jnp.tril(jnp.ones((CHUNK, CHUNK), dtype=bool))

    def step(carry, inp):
        c_mat, n_vec, m = carry
        qc, kc, vc, ic, fc = inp
        b = jnp.cumsum(fc, axis=-1)
        d_log = jnp.where(causal, b[..., :, None] - b[..., None, :] + ic[..., None, :], -jnp.inf)
        inter_log = b + m[..., None]
        m_t = jnp.maximum(inter_log, jnp.max(d_log, axis=-1))
        w_inter = jnp.exp(inter_log - m_t)
        w_intra = jnp.exp(d_log - m_t[..., None])
        s = jnp.einsum('bhid,bhjd->bhij', qc, kc) * w_intra
        num = (jnp.einsum('bhij,bhjv->bhiv', s, vc)
               + w_inter[..., None] * jnp.einsum('bhid,bhdv->bhiv', qc, c_mat))
        nq = jnp.sum(s, axis=-1) + w_inter * jnp.einsum('bhid,bhd->bhi', qc, n_vec)
        h = num / jnp.maximum(jnp.abs(nq), jnp.exp(-m_t))[..., None]
        b_last = b[..., -1]
        m_new = m_t[..., -1]
        w_state = jnp.exp(b_last + m - m_new)
        w_key = jnp.exp(b_last[..., None] - b + ic - m_new[..., None])
        c_mat = w_state[..., None, None] * c_mat + jnp.einsum('bhj,bhjd,bhjv->bhdv', w_key, kc, vc)
        n_vec = w_state[..., None] * n_vec + jnp.einsum('bhj,bhjd->bhd', w_key, kc)
        return (c_mat, n_vec, m_new), h

    carry0 = (jnp.zeros((B, H, dk, dv), jnp.float32),
              jnp.zeros((B, H, dk), jnp.float32),
              jnp.zeros((B, H), jnp.float32))
    xs = (_to_chunks(q), _to_chunks(k), _to_chunks(v), _to_chunks(i_pre), _to_chunks(log_f))
    _, out = lax.scan(step, carry0, xs)
    return _from_chunks(out).astype(v.dtype)


def _mixer(h, w_in, conv_w, conv_b, igate_b, fgate_b, ret_head_norm, mlstm_head_norm, w_out):
    B, S, _ = h.shape
    proj = jnp.einsum('bsd,de->bse', h, w_in)
    sizes = [RET_QK_W, RET_QK_W, RET_V_W, RET_V_W,
             MLSTM_QK_W, MLSTM_QK_W, MLSTM_V_W, MLSTM_V_W, MLSTM_HEADS, MLSTM_HEADS]
    idx = np.cumsum(sizes)[:-1].tolist()
    rq, rk, rv, rg, mq, mk, mv, mo, mi, mf = jnp.split(proj, idx, axis=-1)

    rq = _rotary(rq.reshape(B, S, RET_HEADS, RET_QK_DIM))
    rk = _rotary(rk.reshape(B, S, RET_HEADS, RET_QK_DIM)) * (RET_QK_DIM ** -0.5)
    rv = rv.reshape(B, S, RET_HEADS, RET_V_DIM)
    ret = _retention_chunkwise(rq.transpose(0, 2, 1, 3), rk.transpose(0, 2, 1, 3), rv.transpose(0, 2, 1, 3))
    ret = _head_rmsnorm(ret.transpose(0, 2, 1, 3), ret_head_norm) * jax.nn.silu(rg)

    qk = jax.nn.silu(_causal_conv(jnp.concatenate([mq, mk], axis=-1), conv_w, conv_b))
    mq, mk = jnp.split(qk, 2, axis=-1)
    mq = mq.reshape(B, S, MLSTM_HEADS, MLSTM_QK_DIM).transpose(0, 2, 1, 3)
    mk = (mk * (MLSTM_QK_DIM ** -0.5)).reshape(B, S, MLSTM_HEADS, MLSTM_QK_DIM).transpose(0, 2, 1, 3)
    mv = mv.reshape(B, S, MLSTM_HEADS, MLSTM_V_DIM).transpose(0, 2, 1, 3)
    i_pre = (mi + igate_b).astype(jnp.float32).transpose(0, 2, 1)
    log_f = jax.nn.log_sigmoid((mf + fgate_b).astype(jnp.float32)).transpose(0, 2, 1)
    mh = _mlstm_chunkwise(mq, mk, mv, i_pre, log_f)
    mh = _head_rmsnorm(mh.transpose(0, 2, 1, 3), mlstm_head_norm) * jax.nn.sigmoid(mo)

    return jnp.einsum('bse,ed->bsd', jnp.concatenate([ret, mh], axis=-1), w_out)


def setup_inputs(seed: int = 0) -> dict:
    key = jax.random.key(seed)
    ks = jax.random.split(key, 20)
    f32 = jnp.float32
    L = DEPTH

    def nrm(k, shape, scale):
        return jax.random.normal(k, shape, f32) * scale

    def gain(k, shape):
        return 1.0 + 0.02 * jax.random.normal(k, shape, f32)

    return {
        'x': nrm(ks[0], (BATCH, SEQ, D_MODEL), 1.0),
        'ffn1_norm': gain(ks[1], (L, D_MODEL)),
        'ffn1_w_gate': nrm(ks[2], (L, D_MODEL, D_FF), D_MODEL ** -0.5),
        'ffn1_w_up': nrm(ks[3], (L, D_MODEL, D_FF), D_MODEL ** -0.5),
        'ffn1_w_down': nrm(ks[4], (L, D_FF, D_MODEL), D_FF ** -0.5),
        'mix_norm': gain(ks[5], (L, D_MODEL)),
        'w_in': nrm(ks[6], (L, D_MODEL, D_IN_PROJ), D_MODEL ** -0.5),
        'conv_w': nrm(ks[7], (L, CONV_WIDTH, 2 * MLSTM_QK_W), CONV_WIDTH ** -0.5),
        'conv_b': nrm(ks[8], (L, 2 * MLSTM_QK_W), 0.02),
        'igate_b': nrm(ks[9], (L, MLSTM_HEADS), 0.1),
        'fgate_b': jnp.linspace(3.0, 6.0, MLSTM_HEADS, dtype=f32)[None, :] + nrm(ks[10], (L, MLSTM_HEADS), 0.1),
        'ret_head_norm': gain(ks[11], (L, RET_V_W)),
        'mlstm_head_norm': gain(ks[12], (L, MLSTM_V_W)),
        'w_out': nrm(ks[13], (L, D_MIX, D_MODEL), D_MIX ** -0.5),
        'ffn2_norm': gain(ks[14], (L, D_MODEL)),
        'ffn2_w_gate': nrm(ks[15], (L, D_MODEL, D_FF), D_MODEL ** -0.5),
        'ffn2_w_up': nrm(ks[16], (L, D_MODEL, D_FF), D_MODEL ** -0.5),
        'ffn2_w_down': nrm(ks[17], (L, D_FF, D_MODEL), D_FF ** -0.5),
        'final_norm': gain(ks[18], (D_MODEL,)),
    }


def reference(x, ffn1_norm, ffn1_w_gate, ffn1_w_up, ffn1_w_down, mix_norm, w_in, conv_w, conv_b,
              igate_b, fgate_b, ret_head_norm, mlstm_head_norm, w_out, ffn2_norm, ffn2_w_gate,
              ffn2_w_up, ffn2_w_down, final_norm):
    for layer in range(DEPTH):
        x = x + FFN_RES_WEIGHT * _swiglu(_rmsnorm(x, ffn1_norm[layer]),
                                         ffn1_w_gate[layer], ffn1_w_up[layer], ffn1_w_down[layer])
        x = x + _mixer(_rmsnorm(x, mix_norm[layer]), w_in[layer], conv_w[layer], conv_b[layer],
                       igate_b[layer], fgate_b[layer], ret_head_norm[layer], mlstm_head_norm[layer],
                       w_out[layer])
        x = x + FFN_RES_WEIGHT * _swiglu(_rmsnorm(x, ffn2_norm[layer]),
                                         ffn2_w_gate[layer], ffn2_w_up[layer], ffn2_w_down[layer])
    return _rmsnorm(x, final_norm)
```

```python
import functools

import jax
import jax.numpy as jnp
from jax import lax
from jax.experimental import pallas as pl
from jax.experimental.pallas import tpu as pltpu

F32 = jnp.float32
BF16 = jnp.bfloat16

RET_HEADS = 8
RET_QK_DIM = 128
RET_V_DIM = 256
MLSTM_HEADS = 4
MLSTM_QK_DIM = 256
MLSTM_V_DIM = 512
CONV_WIDTH = 4
CHUNK = 128
ROPE_BASE = 10000.0
NORM_EPS = 1e-6
FFN_RES_WEIGHT = 0.5

RET_QK_W = RET_HEADS * RET_QK_DIM
RET_V_W = RET_HEADS * RET_V_DIM
MLSTM_QK_W = MLSTM_HEADS * MLSTM_QK_DIM
MLSTM_V_W = MLSTM_HEADS * MLSTM_V_DIM
PROJ_W = 2 * RET_QK_W + 2 * RET_V_W + 2 * MLSTM_QK_W + 2 * MLSTM_V_W
N_GATES = 2 * MLSTM_HEADS

LANES = 128
SUBLANES = 8
V7X_VMEM_LIMIT_BYTES = 58 * 1024 * 1024

AUG_W = LANES


def _pick_tile(total, preferred, quantum):
    if total <= preferred:
        return total
    t = (preferred // quantum) * quantum
    while t >= quantum:
        if total % t == 0:
            return t
        t -= quantum
    return total


def _rms_scale(x):
    return lax.rsqrt(jnp.mean(x * x, axis=-1, keepdims=True) + NORM_EPS)


def _ffn_kernel(x_ref, g_ref, wgu_ref, wd_ref, fg_ref, o_ref, xn_ref, *, tf, final_norm):
    f = pl.program_id(1)

    @pl.when(f == 0)
    def _():
        x = x_ref[...]
        xn_ref[...] = (x * _rms_scale(x) * g_ref[...]).astype(BF16)
        o_ref[...] = x

    gu = jnp.dot(xn_ref[...], wgu_ref[...], preferred_element_type=F32)
    g = gu[:, :tf]
    u = gu[:, tf:]
    h = (g * jax.nn.sigmoid(g) * u * FFN_RES_WEIGHT).astype(BF16)
    o_ref[...] += jnp.dot(h, wd_ref[...], preferred_element_type=F32)

    if final_norm:
        @pl.when(f == pl.num_programs(1) - 1)
        def _():
            y = o_ref[...]
            o_ref[...] = y * _rms_scale(y) * fg_ref[...]


def _ffn(x2d, norm_g, w_gate, w_up, w_down, final_g, *, final_norm):
    m, d = x2d.shape
    dff = w_gate.shape[1]
    tm = _pick_tile(m, 512, SUBLANES)
    tf = _pick_tile(dff, 256, LANES)
    nf = dff // tf
    wgu = jnp.concatenate(
        [w_gate.astype(BF16).reshape(d, nf, tf), w_up.astype(BF16).reshape(d, nf, tf)], axis=2
    ).reshape(d, 2 * dff)
    wd = w_down.astype(BF16)
    return pl.pallas_call(
        functools.partial(_ffn_kernel, tf=tf, final_norm=final_norm),
        out_shape=jax.ShapeDtypeStruct((m, d), F32),
        grid=(m // tm, nf),
        in_specs=[
            pl.BlockSpec((tm, d), lambda i, f: (i, 0)),
            pl.BlockSpec((1, d), lambda i, f: (0, 0)),
            pl.BlockSpec((d, 2 * tf), lambda i, f: (0, f)),
            pl.BlockSpec((tf, d), lambda i, f: (f, 0)),
            pl.BlockSpec((1, d), lambda i, f: (0, 0)),
        ],
        out_specs=pl.BlockSpec((tm, d), lambda i, f: (i, 0)),
        scratch_shapes=[pltpu.VMEM((tm, d), BF16)],
        compiler_params=pltpu.CompilerParams(
            dimension_semantics=("parallel", "arbitrary"), vmem_limit_bytes=V7X_VMEM_LIMIT_BYTES
        ),
        name="ffn_final" if final_norm else "ffn",
    )(x2d, norm_g.reshape(1, d), wgu, wd, final_g.reshape(1, d))


def _inproj_kernel(x_ref, g_ref, w_ref, wgt_ref, p_ref, gt_ref, xn_ref):
    n = pl.program_id(1)

    @pl.when(n == 0)
    def _():
        x = x_ref[...]
        xn = (x * _rms_scale(x) * g_ref[...]).astype(BF16)
        xn_ref[...] = xn
        gt = lax.dot_general(wgt_ref[...], xn, (((1,), (1,)), ((), ())), preferred_element_type=F32)
        gt_ref[...] = gt[:SUBLANES, :]

    p_ref[...] = jnp.dot(xn_ref[...], w_ref[...], preferred_element_type=F32).astype(BF16)


def _in_proj(x2d, norm_g, w_in):
    m, d = x2d.shape
    tm = _pick_tile(m, 512, LANES)
    tn = _pick_tile(PROJ_W, 1024, LANES)
    w_main = w_in[:, :PROJ_W].astype(BF16)
    wgt = jnp.zeros((2 * SUBLANES, d), BF16).at[:N_GATES, :].set(w_in[:, PROJ_W:].T.astype(BF16))
    return pl.pallas_call(
        _inproj_kernel,
        out_shape=(
            jax.ShapeDtypeStruct((m, PROJ_W), BF16),
            jax.ShapeDtypeStruct((SUBLANES, m), F32),
        ),
        grid=(m // tm, PROJ_W // tn),
        in_specs=[
            pl.BlockSpec((tm, d), lambda i, n: (i, 0)),
            pl.BlockSpec((1, d), lambda i, n: (0, 0)),
            pl.BlockSpec((d, tn), lambda i, n: (0, n)),
            pl.BlockSpec((2 * SUBLANES, d), lambda i, n: (0, 0)),
        ],
        out_specs=(
            pl.BlockSpec((tm, tn), lambda i, n: (i, n)),
            pl.BlockSpec((SUBLANES, tm), lambda i, n: (0, i)),
        ),
        scratch_shapes=[pltpu.VMEM((tm, d), BF16)],
        compiler_params=pltpu.CompilerParams(
            dimension_semantics=("parallel", "arbitrary"), vmem_limit_bytes=V7X_VMEM_LIMIT_BYTES
        ),
        name="in_proj",
    )(x2d, norm_g.reshape(1, d), w_main, wgt)


def _retention_kernel(lg_ref, q_ref, k_ref, v_ref, g_ref, cos_ref, sin_ref, hn_ref, o_ref, state_ref, *, n_chunks):
    h = pl.program_id(1)
    c = pl.program_id(2)

    @pl.when(c == 0)
    def _():
        state_ref[...] = jnp.zeros_like(state_ref)

    lg = lg_ref[h]
    row = lax.broadcasted_iota(jnp.int32, (CHUNK, CHUNK), 0)
    col = lax.broadcasted_iota(jnp.int32, (CHUNK, CHUNK), 1)
    rel = (row - col).astype(F32)
    intra = jnp.where(rel >= 0, jnp.exp(lg * rel), 0.0)
    pos = lax.broadcasted_iota(jnp.int32, (CHUNK, 1), 0).astype(F32)
    q_decay = jnp.exp(lg * (pos + 1.0))
    k_decay = jnp.exp(lg * (CHUNK - 1.0 - pos))
    chunk_decay = jnp.exp(lg * CHUNK)
    k_scale = RET_QK_DIM ** -0.5
    hn = hn_ref[...]

    for ci in range(n_chunks):
        rows = pl.ds(ci * CHUNK, CHUNK)
        cos = cos_ref[rows, :]
        sin = sin_ref[rows, :]
        q = q_ref[rows, :].astype(F32)
        k = k_ref[rows, :].astype(F32)
        q = q * cos + pltpu.roll(q, RET_QK_DIM // 2, 1) * sin
        k = (k * cos + pltpu.roll(k, RET_QK_DIM // 2, 1) * sin) * k_scale
        qb = q.astype(BF16)
        v = v_ref[rows, :]
        state = state_ref[...]
        scores = lax.dot_general(qb, k.astype(BF16), (((1,), (1,)), ((), ())), preferred_element_type=F32) * intra
        out = jnp.dot(scores.astype(BF16), v, preferred_element_type=F32)
        out = out + jnp.dot(qb, state.astype(BF16), preferred_element_type=F32) * q_decay
        kd_t = (k * k_decay).T.astype(BF16)
        state_ref[...] = state * chunk_decay + jnp.dot(kd_t, v, preferred_element_type=F32)
        gate = g_ref[rows, :].astype(F32)
        y = out * _rms_scale(out) * hn * (gate * jax.nn.sigmoid(gate))
        o_ref[rows, :] = y.astype(BF16)


def _retention(proj, log_gamma, cos2, sin2, head_norm, batch, seq):
    m = proj.shape[0]
    tr = _pick_tile(seq, 512, CHUNK)
    tiles = seq // tr
    q_blk, k_blk = 0, RET_QK_W // RET_QK_DIM
    v_blk = (2 * RET_QK_W) // RET_V_DIM
    g_blk = (2 * RET_QK_W + RET_V_W) // RET_V_DIM
    return pl.pallas_call(
        functools.partial(_retention_kernel, n_chunks=tr // CHUNK),
        out_shape=jax.ShapeDtypeStruct((m, RET_V_W), BF16),
        grid=(batch, RET_HEADS, tiles),
        in_specs=[
            pl.BlockSpec(memory_space=pltpu.SMEM),
            pl.BlockSpec((tr, RET_QK_DIM), lambda b, h, c: (b * tiles + c, q_blk + h)),
            pl.BlockSpec((tr, RET_QK_DIM), lambda b, h, c: (b * tiles + c, k_blk + h)),
            pl.BlockSpec((tr, RET_V_DIM), lambda b, h, c: (b * tiles + c, v_blk + h)),
            pl.BlockSpec((tr, RET_V_DIM), lambda b, h, c: (b * tiles + c, g_blk + h)),
            pl.BlockSpec((tr, RET_QK_DIM), lambda b, h, c: (c, 0)),
            pl.BlockSpec((tr, RET_QK_DIM), lambda b, h, c: (c, 0)),
            pl.BlockSpec((1, RET_V_DIM), lambda b, h, c: (0, h)),
        ],
        out_specs=pl.BlockSpec((tr, RET_V_DIM), lambda b, h, c: (b * tiles + c, h)),
        scratch_shapes=[pltpu.VMEM((RET_QK_DIM, RET_V_DIM), F32)],
        compiler_params=pltpu.CompilerParams(
            dimension_semantics=("parallel", "parallel", "arbitrary"), vmem_limit_bytes=V7X_VMEM_LIMIT_BYTES
        ),
        name="retention",
    )(log_gamma, proj, proj, proj, proj, cos2, sin2, head_norm.reshape(1, RET_V_W))


def _log_sigmoid(x):
    return jnp.minimum(x, 0.0) - jnp.log1p(jnp.exp(-jnp.abs(x)))


def _mlstm_kernel(ib_ref, fb_ref, q_ref, k_ref, v_ref, og_ref, gt_ref, cwq_ref, cwk_ref, cbq_ref, cbk_ref, hn_ref,
                  o_ref, c_ref, m_ref, qraw_ref, kraw_ref, qs_ref, ks_ref, *, n_chunks):
    h = pl.program_id(1)
    c = pl.program_id(2)
    tr = n_chunks * CHUNK
    hist = SUBLANES

    @pl.when(c == 0)
    def _():
        c_ref[...] = jnp.zeros_like(c_ref)
        m_ref[...] = jnp.zeros_like(m_ref)
        qraw_ref[pl.ds(0, hist), :] = jnp.zeros((hist, MLSTM_QK_DIM), F32)
        kraw_ref[pl.ds(0, hist), :] = jnp.zeros((hist, MLSTM_QK_DIM), F32)

    def conv_silu(raw_ref, src_ref, w_ref, b_ref):
        raw_ref[pl.ds(hist, tr), :] = src_ref[...].astype(F32)
        acc = jnp.broadcast_to(b_ref[...], (tr, MLSTM_QK_DIM))
        for j in range(CONV_WIDTH):
            acc = acc + raw_ref[pl.ds(hist - (CONV_WIDTH - 1) + j, tr), :] * w_ref[pl.ds(j, 1), :]
        raw_ref[pl.ds(0, hist), :] = raw_ref[pl.ds(tr, hist), :]
        return acc * jax.nn.sigmoid(acc)

    qs_ref[...] = conv_silu(qraw_ref, q_ref, cwq_ref, cbq_ref).astype(BF16)
    ks_ref[...] = (conv_silu(kraw_ref, k_ref, cwk_ref, cbk_ref) * (MLSTM_QK_DIM ** -0.5)).astype(BF16)

    ib = ib_ref[h]
    fb = fb_ref[h]
    hn = hn_ref[...]
    row = lax.broadcasted_iota(jnp.int32, (CHUNK, CHUNK), 0)
    col = lax.broadcasted_iota(jnp.int32, (CHUNK, CHUNK), 1)
    causal = col <= row
    lane = lax.broadcasted_iota(jnp.int32, (1, CHUNK), 1)
    gate_row = lax.broadcasted_iota(jnp.int32, (SUBLANES, CHUNK), 0)
    ones_col = (lax.broadcasted_iota(jnp.int32, (CHUNK, AUG_W), 1) == 0).astype(BF16)

    for ci in range(n_chunks):
        rows = pl.ds(ci * CHUNK, CHUNK)
        gates = gt_ref[:, rows]
        i_row = jnp.sum(jnp.where(gate_row == h, gates, 0.0), axis=0, keepdims=True) + ib
        f_row = jnp.sum(jnp.where(gate_row == MLSTM_HEADS + h, gates, 0.0), axis=0, keepdims=True) + fb
        lf_row = _log_sigmoid(f_row)
        b_row = lf_row
        shift = 1
        while shift < CHUNK:
            b_row = b_row + jnp.where(lane >= shift, pltpu.roll(b_row, shift, 1), 0.0)
            shift *= 2
        stacked = jnp.where(row == 0, b_row, jnp.where(row == 1, i_row, 0.0))
        stacked_t = stacked.T
        b_col = stacked_t[:, 0:1]
        i_col = stacked_t[:, 1:2]

        m_prev = m_ref[:, 0:1]
        d_log = jnp.where(causal, b_col - b_row + i_row, -jnp.inf)
        inter = b_col + m_prev
        m_t = jnp.maximum(inter, jnp.max(d_log, axis=1, keepdims=True))
        w_inter = jnp.exp(inter - m_t)
        w_intra = jnp.exp(d_log - m_t)

        q = qs_ref[rows, :]
        k = ks_ref[rows, :]
        v_aug = jnp.concatenate([v_ref[rows, :], ones_col], axis=1)
        state = c_ref[...]
        s = lax.dot_general(q, k, (((1,), (1,)), ((), ())), preferred_element_type=F32) * w_intra
        tot = jnp.dot(s.astype(BF16), v_aug, preferred_element_type=F32)
        tot = tot + w_inter * jnp.dot(q, state.astype(BF16), preferred_element_type=F32)
        num = tot[:, :MLSTM_V_DIM]
        nq = tot[:, MLSTM_V_DIM:MLSTM_V_DIM + 1]
        hid = num / jnp.maximum(jnp.abs(nq), jnp.exp(-m_t))

        b_last = b_row[:, CHUNK - 1:CHUNK]
        m_new = m_t[CHUNK - 1:CHUNK, :]
        w_state = jnp.exp(b_last + m_prev - m_new)
        w_key = jnp.exp(b_last - b_col + i_col - m_new)
        kw_t = (k.astype(F32) * w_key).T.astype(BF16)
        c_ref[...] = w_state * state + jnp.dot(kw_t, v_aug, preferred_element_type=F32)
        m_ref[...] = jnp.broadcast_to(m_new, m_ref.shape)

        og = og_ref[rows, :].astype(F32)
        y = hid * _rms_scale(hid) * hn * jax.nn.sigmoid(og)
        o_ref[rows, :] = y.astype(BF16)


def _mlstm(proj, gates_t, conv_w, conv_b, igate_b, fgate_b, head_norm, batch, seq):
    m = proj.shape[0]
    tr = _pick_tile(seq, 512, CHUNK)
    tiles = seq // tr
    base = 2 * RET_QK_W + 2 * RET_V_W
    q_blk = base // MLSTM_QK_DIM
    k_blk = (base + MLSTM_QK_W) // MLSTM_QK_DIM
    v_blk = (base + 2 * MLSTM_QK_W) // MLSTM_V_DIM
    o_blk = (base + 2 * MLSTM_QK_W + MLSTM_V_W) // MLSTM_V_DIM
    smem = pl.BlockSpec(memory_space=pltpu.SMEM)
    return pl.pallas_call(
        functools.partial(_mlstm_kernel, n_chunks=tr // CHUNK),
        out_shape=jax.ShapeDtypeStruct((m, MLSTM_V_W), BF16),
        grid=(batch, MLSTM_HEADS, tiles),
        in_specs=[
            smem,
            smem,
            pl.BlockSpec((tr, MLSTM_QK_DIM), lambda b, h, c: (b * tiles + c, q_blk + h)),
            pl.BlockSpec((tr, MLSTM_QK_DIM), lambda b, h, c: (b * tiles + c, k_blk + h)),
            pl.BlockSpec((tr, MLSTM_V_DIM), lambda b, h, c: (b * tiles + c, v_blk + h)),
            pl.BlockSpec((tr, MLSTM_V_DIM), lambda b, h, c: (b * tiles + c, o_blk + h)),
            pl.BlockSpec((SUBLANES, tr), lambda b, h, c: (0, b * tiles + c)),
            pl.BlockSpec((CONV_WIDTH, MLSTM_QK_DIM), lambda b, h, c: (0, h)),
            pl.BlockSpec((CONV_WIDTH, MLSTM_QK_DIM), lambda b, h, c: (0, MLSTM_HEADS + h)),
            pl.BlockSpec((1, MLSTM_QK_DIM), lambda b, h, c: (0, h)),
            pl.BlockSpec((1, MLSTM_QK_DIM), lambda b, h, c: (0, MLSTM_HEADS + h)),
            pl.BlockSpec((1, MLSTM_V_DIM), lambda b, h, c: (0, h)),
        ],
        out_specs=pl.BlockSpec((tr, MLSTM_V_DIM), lambda b, h, c: (b * tiles + c, h)),
        scratch_shapes=[
            pltpu.VMEM((MLSTM_QK_DIM, MLSTM_V_DIM + AUG_W), F32),
            pltpu.VMEM((1, LANES), F32),
            pltpu.VMEM((tr + SUBLANES, MLSTM_QK_DIM), F32),
            pltpu.VMEM((tr + SUBLANES, MLSTM_QK_DIM), F32),
            pltpu.VMEM((tr, MLSTM_QK_DIM), BF16),
            pltpu.VMEM((tr, MLSTM_QK_DIM), BF16),
        ],
        compiler_params=pltpu.CompilerParams(
            dimension_semantics=("parallel", "parallel", "arbitrary"), vmem_limit_bytes=V7X_VMEM_LIMIT_BYTES
        ),
        name="mlstm",
    )(igate_b, fgate_b, proj, proj, proj, proj, gates_t, conv_w, conv_w, conv_b.reshape(1, -1),
      conv_b.reshape(1, -1), head_norm.reshape(1, MLSTM_V_W))


def _outproj_kernel(x_ref, r_ref, h_ref, wr_ref, wh_ref, o_ref):
    acc = jnp.dot(r_ref[...], wr_ref[...], preferred_element_type=F32)
    acc = acc + jnp.dot(h_ref[...], wh_ref[...], preferred_element_type=F32)
    o_ref[...] = x_ref[...] + acc


def _out_proj(x2d, ret, mh, w_out):
    m, d = x2d.shape
    tm = _pick_tile(m, 512, SUBLANES)
    tn = _pick_tile(d, 1024, LANES)
    wo = w_out.astype(BF16)
    return pl.pallas_call(
        _outproj_kernel,
        out_shape=jax.ShapeDtypeStruct((m, d), F32),
        grid=(m // tm, d // tn),
        in_specs=[
            pl.BlockSpec((tm, tn), lambda i, n: (i, n)),
            pl.BlockSpec((tm, RET_V_W), lambda i, n: (i, 0)),
            pl.BlockSpec((tm, MLSTM_V_W), lambda i, n: (i, 0)),
            pl.BlockSpec((RET_V_W, tn), lambda i, n: (0, n)),
            pl.BlockSpec((MLSTM_V_W, tn), lambda i, n: (1, n)),
        ],
        out_specs=pl.BlockSpec((tm, tn), lambda i, n: (i, n)),
        compiler_params=pltpu.CompilerParams(
            dimension_semantics=("parallel", "parallel"), vmem_limit_bytes=V7X_VMEM_LIMIT_BYTES
        ),
        name="out_proj",
    )(x2d, ret, mh, wo, wo)


def _rotary_tables(seq):
    half = RET_QK_DIM // 2
    inv_freq = ROPE_BASE ** (-jnp.arange(half, dtype=F32) / half)
    ang = jnp.arange(seq, dtype=F32)[:, None] * inv_freq[None, :]
    cos, sin = jnp.cos(ang), jnp.sin(ang)
    return jnp.concatenate([cos, cos], axis=1), jnp.concatenate([-sin, sin], axis=1)


def kernel(x, ffn1_norm, ffn1_w_gate, ffn1_w_up, ffn1_w_down, mix_norm, w_in, conv_w, conv_b, igate_b, fgate_b,
           ret_head_norm, mlstm_head_norm, w_out, ffn2_norm, ffn2_w_gate, ffn2_w_up, ffn2_w_down, final_norm):
    batch, seq, d = x.shape
    depth = ffn1_norm.shape[0]
    assert seq % CHUNK == 0 and w_in.shape[-1] == PROJ_W + N_GATES and w_out.shape[1] == RET_V_W + MLSTM_V_W
    log_gamma = jnp.log1p(-jnp.exp2(-5.0 - jnp.arange(RET_HEADS, dtype=F32)))
    cos2, sin2 = _rotary_tables(seq)
    h = x.reshape(batch * seq, d)
    for layer in range(depth):
        last = layer == depth - 1
        h = _ffn(h, ffn1_norm[layer], ffn1_w_gate[layer], ffn1_w_up[layer], ffn1_w_down[layer], final_norm,
                 final_norm=False)
        proj, gates_t = _in_proj(h, mix_norm[layer], w_in[layer])
        ret = _retention(proj, log_gamma, cos2, sin2, ret_head_norm[layer], batch, seq)
        mh = _mlstm(proj, gates_t, conv_w[layer], conv_b[layer], igate_b[layer], fgate_b[layer],
                    mlstm_head_norm[layer], batch, seq)
        h = _out_proj(h, ret, mh, w_out[layer])
        h = _ffn(h, ffn2_norm[layer], ffn2_w_gate[layer], ffn2_w_up[layer], ffn2_w_down[layer], final_norm,
                 final_norm=last)
    if depth == 0:
        raise ValueError("depth must be >= 1")
    return h.reshape(batch, seq, d)
```

```python
import functools

import jax
import jax.numpy as jnp
from jax import lax
from jax.experimental import pallas as pl
from jax.experimental.pallas import tpu as pltpu

F32 = jnp.float32
BF16 = jnp.bfloat16

RET_HEADS = 8
RET_QK_DIM = 128
RET_V_DIM = 256
MLSTM_HEADS = 4
MLSTM_QK_DIM = 256
MLSTM_V_DIM = 512
CONV_WIDTH = 4
CHUNK = 128
ROPE_BASE = 10000.0
NORM_EPS = 1e-6
FFN_RES_WEIGHT = 0.5

RET_QK_W = RET_HEADS * RET_QK_DIM
RET_V_W = RET_HEADS * RET_V_DIM
MLSTM_QK_W = MLSTM_HEADS * MLSTM_QK_DIM
MLSTM_V_W = MLSTM_HEADS * MLSTM_V_DIM
PROJ_W = 2 * RET_QK_W + 2 * RET_V_W + 2 * MLSTM_QK_W + 2 * MLSTM_V_W
N_GATES = 2 * MLSTM_HEADS

LANES = 128
SUBLANES = 8
V7X_VMEM_LIMIT_BYTES = 58 * 1024 * 1024

AUG_W = LANES


def _pick_tile(total, preferred, quantum):
    if total <= preferred:
        return total
    t = (preferred // quantum) * quantum
    while t >= quantum:
        if total % t == 0:
            return t
        t -= quantum
    return total


def _rms_scale(x):
    return lax.rsqrt(jnp.mean(x * x, axis=-1, keepdims=True) + NORM_EPS)


def _ffn_kernel(x_ref, g_ref, wg_ref, wu_ref, wd_ref, fg_ref, o_ref, xn_ref, *, tf, final_norm):
    f = pl.program_id(1)

    @pl.when(f == 0)
    def _():
        x = x_ref[...]
        xn_ref[...] = (x * _rms_scale(x) * g_ref[...]).astype(BF16)
        o_ref[...] = x

    wgu = jnp.concatenate([wg_ref[...], wu_ref[...]], axis=1)
    gu = jnp.dot(xn_ref[...], wgu, preferred_element_type=F32)
    g = gu[:, :tf]
    u = gu[:, tf:]
    h = (g * jax.nn.sigmoid(g) * u * FFN_RES_WEIGHT).astype(BF16)
    o_ref[...] += jnp.dot(h, wd_ref[...], preferred_element_type=F32)

    if final_norm:
        @pl.when(f == pl.num_programs(1) - 1)
        def _():
            y = o_ref[...]
            o_ref[...] = y * _rms_scale(y) * fg_ref[...]


def _ffn(x2d, norm_g, w_gate, w_up, w_down, final_g, *, final_norm):
    m, d = x2d.shape
    dff = w_gate.shape[1]
    tm = _pick_tile(m, 512, SUBLANES)
    tf = _pick_tile(dff, 256, LANES)
    nf = dff // tf
    return pl.pallas_call(
        functools.partial(_ffn_kernel, tf=tf, final_norm=final_norm),
        out_shape=jax.ShapeDtypeStruct((m, d), F32),
        grid=(m // tm, nf),
        in_specs=[
            pl.BlockSpec((tm, d), lambda i, f: (i, 0)),
            pl.BlockSpec((1, d), lambda i, f: (0, 0)),
            pl.BlockSpec((d, tf), lambda i, f: (0, f)),
            pl.BlockSpec((d, tf), lambda i, f: (0, f)),
            pl.BlockSpec((tf, d), lambda i, f: (f, 0)),
            pl.BlockSpec((1, d), lambda i, f: (0, 0)),
        ],
        out_specs=pl.BlockSpec((tm, d), lambda i, f: (i, 0)),
        scratch_shapes=[pltpu.VMEM((tm, d), BF16)],
        compiler_params=pltpu.CompilerParams(
            dimension_semantics=("parallel", "arbitrary"), vmem_limit_bytes=V7X_VMEM_LIMIT_BYTES
        ),
        name="ffn_final" if final_norm else "ffn",
    )(x2d, norm_g.reshape(1, d), w_gate.astype(BF16), w_up.astype(BF16), w_down.astype(BF16), final_g.reshape(1, d))


def _inproj_kernel(x_ref, g_ref, wt_ref, wgt_ref, p_ref, gt_ref, xn_ref):
    n = pl.program_id(1)
    trans_b = (((1,), (1,)), ((), ()))

    @pl.when(n == 0)
    def _():
        x = x_ref[...]
        xn = (x * _rms_scale(x) * g_ref[...]).astype(BF16)
        xn_ref[...] = xn
        wg = jnp.concatenate([wgt_ref[...], jnp.zeros_like(wgt_ref)], axis=0).astype(BF16)
        gt = lax.dot_general(wg, xn, trans_b, preferred_element_type=F32)
        gt_ref[...] = gt[:SUBLANES, :]

    p_ref[...] = lax.dot_general(xn_ref[...], wt_ref[...], trans_b, preferred_element_type=F32).astype(BF16)


def _in_proj(x2d, norm_g, w_in):
    m, d = x2d.shape
    assert N_GATES == SUBLANES and PROJ_W % SUBLANES == 0
    tm = _pick_tile(m, 512, LANES)
    tn = _pick_tile(PROJ_W, 1024, LANES)
    w_t = w_in.T
    return pl.pallas_call(
        _inproj_kernel,
        out_shape=(
            jax.ShapeDtypeStruct((m, PROJ_W), BF16),
            jax.ShapeDtypeStruct((SUBLANES, m), F32),
        ),
        grid=(m // tm, PROJ_W // tn),
        in_specs=[
            pl.BlockSpec((tm, d), lambda i, n: (i, 0)),
            pl.BlockSpec((1, d), lambda i, n: (0, 0)),
            pl.BlockSpec((tn, d), lambda i, n: (n, 0)),
            pl.BlockSpec((N_GATES, d), lambda i, n: (PROJ_W // N_GATES, 0)),
        ],
        out_specs=(
            pl.BlockSpec((tm, tn), lambda i, n: (i, n)),
            pl.BlockSpec((SUBLANES, tm), lambda i, n: (0, i)),
        ),
        scratch_shapes=[pltpu.VMEM((tm, d), BF16)],
        compiler_params=pltpu.CompilerParams(
            dimension_semantics=("parallel", "arbitrary"), vmem_limit_bytes=V7X_VMEM_LIMIT_BYTES
        ),
        name="in_proj",
    )(x2d, norm_g.reshape(1, d), w_t.astype(BF16), w_t)


def _retention_kernel(lg_ref, q_ref, k_ref, v_ref, g_ref, cos_ref, sin_ref, hn_ref, o_ref, state_ref, *, n_chunks):
    h = pl.program_id(1)
    c = pl.program_id(2)

    @pl.when(c == 0)
    def _():
        state_ref[...] = jnp.zeros_like(state_ref)

    lg = lg_ref[h]
    row = lax.broadcasted_iota(jnp.int32, (CHUNK, CHUNK), 0)
    col = lax.broadcasted_iota(jnp.int32, (CHUNK, CHUNK), 1)
    rel = (row - col).astype(F32)
    intra = jnp.where(rel >= 0, jnp.exp(lg * rel), 0.0)
    pos = lax.broadcasted_iota(jnp.int32, (CHUNK, 1), 0).astype(F32)
    q_decay = jnp.exp(lg * (pos + 1.0))
    k_decay = jnp.exp(lg * (CHUNK - 1.0 - pos))
    chunk_decay = jnp.exp(lg * CHUNK)
    k_scale = RET_QK_DIM ** -0.5
    hn = hn_ref[...]

    for ci in range(n_chunks):
        rows = pl.ds(ci * CHUNK, CHUNK)
        cos = cos_ref[rows, :]
        sin = sin_ref[rows, :]
        q = q_ref[rows, :].astype(F32)
        k = k_ref[rows, :].astype(F32)
        q = q * cos + pltpu.roll(q, RET_QK_DIM // 2, 1) * sin
        k = (k * cos + pltpu.roll(k, RET_QK_DIM // 2, 1) * sin) * k_scale
        qb = q.astype(BF16)
        v = v_ref[rows, :]
        state = state_ref[...]
        scores = lax.dot_general(qb, k.astype(BF16), (((1,), (1,)), ((), ())), preferred_element_type=F32) * intra
        out = jnp.dot(scores.astype(BF16), v, preferred_element_type=F32)
        out = out + jnp.dot(qb, state.astype(BF16), preferred_element_type=F32) * q_decay
        kd_t = (k * k_decay).T.astype(BF16)
        state_ref[...] = state * chunk_decay + jnp.dot(kd_t, v, preferred_element_type=F32)
        gate = g_ref[rows, :].astype(F32)
        y = out * _rms_scale(out) * hn * (gate * jax.nn.sigmoid(gate))
        o_ref[rows, :] = y.astype(BF16)


def _retention(proj, log_gamma, cos2, sin2, head_norm, batch, seq):
    m = proj.shape[0]
    tr = _pick_tile(seq, 512, CHUNK)
    tiles = seq // tr
    q_blk, k_blk = 0, RET_QK_W // RET_QK_DIM
    v_blk = (2 * RET_QK_W) // RET_V_DIM
    g_blk = (2 * RET_QK_W + RET_V_W) // RET_V_DIM
    return pl.pallas_call(
        functools.partial(_retention_kernel, n_chunks=tr // CHUNK),
        out_shape=jax.ShapeDtypeStruct((m, RET_V_W), BF16),
        grid=(batch, RET_HEADS, tiles),
        in_specs=[
            pl.BlockSpec(memory_space=pltpu.SMEM),
            pl.BlockSpec((tr, RET_QK_DIM), lambda b, h, c: (b * tiles + c, q_blk + h)),
            pl.BlockSpec((tr, RET_QK_DIM), lambda b, h, c: (b * tiles + c, k_blk + h)),
            pl.BlockSpec((tr, RET_V_DIM), lambda b, h, c: (b * tiles + c, v_blk + h)),
            pl.BlockSpec((tr, RET_V_DIM), lambda b, h, c: (b * tiles + c, g_blk + h)),
            pl.BlockSpec((tr, RET_QK_DIM), lambda b, h, c: (c, 0)),
            pl.BlockSpec((tr, RET_QK_DIM), lambda b, h, c: (c, 0)),
            pl.BlockSpec((1, RET_V_DIM), lambda b, h, c: (0, h)),
        ],
        out_specs=pl.BlockSpec((tr, RET_V_DIM), lambda b, h, c: (b * tiles + c, h)),
        scratch_shapes=[pltpu.VMEM((RET_QK_DIM, RET_V_DIM), F32)],
        compiler_params=pltpu.CompilerParams(
            dimension_semantics=("parallel", "parallel", "arbitrary"), vmem_limit_bytes=V7X_VMEM_LIMIT_BYTES
        ),
        name="retention",
    )(log_gamma, proj, proj, proj, proj, cos2, sin2, head_norm.reshape(1, RET_V_W))


def _log_sigmoid(x):
    return jnp.minimum(x, 0.0) - jnp.log1p(jnp.exp(-jnp.abs(x)))


def _mlstm_kernel(ib_ref, fb_ref, q_ref, k_ref, v_ref, og_ref, gt_ref, cwq_ref, cwk_ref, cbq_ref, cbk_ref, hn_ref,
                  o_ref, c_ref, m_ref, qraw_ref, kraw_ref, qs_ref, ks_ref, *, n_chunks):
    h = pl.program_id(1)
    c = pl.program_id(2)
    tr = n_chunks * CHUNK
    hist = SUBLANES

    @pl.when(c == 0)
    def _():
        c_ref[...] = jnp.zeros_like(c_ref)
        m_ref[...] = jnp.zeros_like(m_ref)
        qraw_ref[pl.ds(0, hist), :] = jnp.zeros((hist, MLSTM_QK_DIM), F32)
        kraw_ref[pl.ds(0, hist), :] = jnp.zeros((hist, MLSTM_QK_DIM), F32)

    def conv_silu(raw_ref, src_ref, w_ref, b_ref):
        raw_ref[pl.ds(hist, tr), :] = src_ref[...].astype(F32)
        acc = jnp.broadcast_to(b_ref[...], (tr, MLSTM_QK_DIM))
        for j in range(CONV_WIDTH):
            acc = acc + raw_ref[pl.ds(hist - (CONV_WIDTH - 1) + j, tr), :] * w_ref[pl.ds(j, 1), :]
        raw_ref[pl.ds(0, hist), :] = raw_ref[pl.ds(tr, hist), :]
        return acc * jax.nn.sigmoid(acc)

    qs_ref[...] = conv_silu(qraw_ref, q_ref, cwq_ref, cbq_ref).astype(BF16)
    ks_ref[...] = (conv_silu(kraw_ref, k_ref, cwk_ref, cbk_ref) * (MLSTM_QK_DIM ** -0.5)).astype(BF16)

    ib = ib_ref[h]
    fb = fb_ref[h]
    hn = hn_ref[...]
    row = lax.broadcasted_iota(jnp.int32, (CHUNK, CHUNK), 0)
    col = lax.broadcasted_iota(jnp.int32, (CHUNK, CHUNK), 1)
    causal = col <= row
    lane = lax.broadcasted_iota(jnp.int32, (1, CHUNK), 1)
    gate_row = lax.broadcasted_iota(jnp.int32, (SUBLANES, CHUNK), 0)
    ones_col = (lax.broadcasted_iota(jnp.int32, (CHUNK, AUG_W), 1) == 0).astype(BF16)

    for ci in range(n_chunks):
        rows = pl.ds(ci * CHUNK, CHUNK)
        gates = gt_ref[:, rows]
        i_row = jnp.sum(jnp.where(gate_row == h, gates, 0.0), axis=0, keepdims=True) + ib
        f_row = jnp.sum(jnp.where(gate_row == MLSTM_HEADS + h, gates, 0.0), axis=0, keepdims=True) + fb
        lf_row = _log_sigmoid(f_row)
        b_row = lf_row
        shift = 1
        while shift < CHUNK:
            b_row = b_row + jnp.where(lane >= shift, pltpu.roll(b_row, shift, 1), 0.0)
            shift *= 2
        stacked = jnp.where(row == 0, b_row, jnp.where(row == 1, i_row, 0.0))
        stacked_t = stacked.T
        b_col = stacked_t[:, 0:1]
        i_col = stacked_t[:, 1:2]

        m_prev = m_ref[:, 0:1]
        d_log = jnp.where(causal, b_col - b_row + i_row, -jnp.inf)
        inter = b_col + m_prev
        m_t = jnp.maximum(inter, jnp.max(d_log, axis=1, keepdims=True))
        w_inter = jnp.exp(inter - m_t)
        w_intra = jnp.exp(d_log - m_t)

        q = qs_ref[rows, :]
        k = ks_ref[rows, :]
        v_aug = jnp.concatenate([v_ref[rows, :], ones_col], axis=1)
        state = c_ref[...]
        s = lax.dot_general(q, k, (((1,), (1,)), ((), ())), preferred_element_type=F32) * w_intra
        tot = jnp.dot(s.astype(BF16), v_aug, preferred_element_type=F32)
        tot = tot + w_inter * jnp.dot(q, state.astype(BF16), preferred_element_type=F32)
        num = tot[:, :MLSTM_V_DIM]
        nq = tot[:, MLSTM_V_DIM:MLSTM_V_DIM + 1]
        hid = num / jnp.maximum(jnp.abs(nq), jnp.exp(-m_t))

        b_last = b_row[:, CHUNK - 1:CHUNK]
        m_new = m_t[CHUNK - 1:CHUNK, :]
        w_state = jnp.exp(b_last + m_prev - m_new)
        w_key = jnp.exp(b_last - b_col + i_col - m_new)
        kw_t = (k.astype(F32) * w_key).T.astype(BF16)
        c_ref[...] = w_state * state + jnp.dot(kw_t, v_aug, preferred_element_type=F32)
        m_ref[...] = jnp.broadcast_to(m_new, m_ref.shape)

        og = og_ref[rows, :].astype(F32)
        y = hid * _rms_scale(hid) * hn * jax.nn.sigmoid(og)
        o_ref[rows, :] = y.astype(BF16)


def _mlstm(proj, gates_t, conv_w, conv_b, igate_b, fgate_b, head_norm, batch, seq):
    m = proj.shape[0]
    tr = _pick_tile(seq, 512, CHUNK)
    tiles = seq // tr
    base = 2 * RET_QK_W + 2 * RET_V_W
    q_blk = base // MLSTM_QK_DIM
    k_blk = (base + MLSTM_QK_W) // MLSTM_QK_DIM
    v_blk = (base + 2 * MLSTM_QK_W) // MLSTM_V_DIM
    o_blk = (base + 2 * MLSTM_QK_W + MLSTM_V_W) // MLSTM_V_DIM
    smem = pl.BlockSpec(memory_space=pltpu.SMEM)
    return pl.pallas_call(
        functools.partial(_mlstm_kernel, n_chunks=tr // CHUNK),
        out_shape=jax.ShapeDtypeStruct((m, MLSTM_V_W), BF16),
        grid=(batch, MLSTM_HEADS, tiles),
        in_specs=[
            smem,
            smem,
            pl.BlockSpec((tr, MLSTM_QK_DIM), lambda b, h, c: (b * tiles + c, q_blk + h)),
            pl.BlockSpec((tr, MLSTM_QK_DIM), lambda b, h, c: (b * tiles + c, k_blk + h)),
            pl.BlockSpec((tr, MLSTM_V_DIM), lambda b, h, c: (b * tiles + c, v_blk + h)),
            pl.BlockSpec((tr, MLSTM_V_DIM), lambda b, h, c: (b * tiles + c, o_blk + h)),
            pl.BlockSpec((SUBLANES, tr), lambda b, h, c: (0, b * tiles + c)),
            pl.BlockSpec((CONV_WIDTH, MLSTM_QK_DIM), lambda b, h, c: (0, h)),
            pl.BlockSpec((CONV_WIDTH, MLSTM_QK_DIM), lambda b, h, c: (0, MLSTM_HEADS + h)),
            pl.BlockSpec((1, MLSTM_QK_DIM), lambda b, h, c: (0, h)),
            pl.BlockSpec((1, MLSTM_QK_DIM), lambda b, h, c: (0, MLSTM_HEADS + h)),
            pl.BlockSpec((1, MLSTM_V_DIM), lambda b, h, c: (0, h)),
        ],
        out_specs=pl.BlockSpec((tr, MLSTM_V_DIM), lambda b, h, c: (b * tiles + c, h)),
        scratch_shapes=[
            pltpu.VMEM((MLSTM_QK_DIM, MLSTM_V_DIM + AUG_W), F32),
            pltpu.VMEM((1, LANES), F32),
            pltpu.VMEM((tr + SUBLANES, MLSTM_QK_DIM), F32),
            pltpu.VMEM((tr + SUBLANES, MLSTM_QK_DIM), F32),
            pltpu.VMEM((tr, MLSTM_QK_DIM), BF16),
            pltpu.VMEM((tr, MLSTM_QK_DIM), BF16),
        ],
        compiler_params=pltpu.CompilerParams(
            dimension_semantics=("parallel", "parallel", "arbitrary"), vmem_limit_bytes=V7X_VMEM_LIMIT_BYTES
        ),
        name="mlstm",
    )(igate_b, fgate_b, proj, proj, proj, proj, gates_t, conv_w, conv_w, conv_b.reshape(1, -1),
      conv_b.reshape(1, -1), head_norm.reshape(1, MLSTM_V_W))


def _outproj_kernel(x_ref, r_ref, h_ref, wr_ref, wh_ref, o_ref):
    acc = jnp.dot(r_ref[...], wr_ref[...], preferred_element_type=F32)
    acc = acc + jnp.dot(h_ref[...], wh_ref[...], preferred_element_type=F32)
    o_ref[...] = x_ref[...] + acc


def _out_proj(x2d, ret, mh, w_out):
    m, d = x2d.shape
    tm = _pick_tile(m, 512, SUBLANES)
    tn = _pick_tile(d, 1024, LANES)
    wo = w_out.astype(BF16)
    return pl.pallas_call(
        _outproj_kernel,
        out_shape=jax.ShapeDtypeStruct((m, d), F32),
        grid=(m // tm, d // tn),
        in_specs=[
            pl.BlockSpec((tm, tn), lambda i, n: (i, n)),
            pl.BlockSpec((tm, RET_V_W), lambda i, n: (i, 0)),
            pl.BlockSpec((tm, MLSTM_V_W), lambda i, n: (i, 0)),
            pl.BlockSpec((RET_V_W, tn), lambda i, n: (0, n)),
            pl.BlockSpec((MLSTM_V_W, tn), lambda i, n: (1, n)),
        ],
        out_specs=pl.BlockSpec((tm, tn), lambda i, n: (i, n)),
        compiler_params=pltpu.CompilerParams(
            dimension_semantics=("parallel", "parallel"), vmem_limit_bytes=V7X_VMEM_LIMIT_BYTES
        ),
        name="out_proj",
    )(x2d, ret, mh, wo, wo)


def _rotary_tables(seq):
    half = RET_QK_DIM // 2
    inv_freq = ROPE_BASE ** (-jnp.arange(half, dtype=F32) / half)
    ang = jnp.arange(seq, dtype=F32)[:, None] * inv_freq[None, :]
    cos, sin = jnp.cos(ang), jnp.sin(ang)
    return jnp.concatenate([cos, cos], axis=1), jnp.concatenate([-sin, sin], axis=1)


def kernel(x, ffn1_norm, ffn1_w_gate, ffn1_w_up, ffn1_w_down, mix_norm, w_in, conv_w, conv_b, igate_b, fgate_b,
           ret_head_norm, mlstm_head_norm, w_out, ffn2_norm, ffn2_w_gate, ffn2_w_up, ffn2_w_down, final_norm):
    batch, seq, d = x.shape
    depth = ffn1_norm.shape[0]
    assert seq % CHUNK == 0 and w_in.shape[-1] == PROJ_W + N_GATES and w_out.shape[1] == RET_V_W + MLSTM_V_W
    log_gamma = jnp.log1p(-jnp.exp2(-5.0 - jnp.arange(RET_HEADS, dtype=F32)))
    cos2, sin2 = _rotary_tables(seq)
    h = x.reshape(batch * seq, d)
    for layer in range(depth):
        last = layer == depth - 1
        h = _ffn(h, ffn1_norm[layer], ffn1_w_gate[layer], ffn1_w_up[layer], ffn1_w_down[layer], final_norm,
                 final_norm=False)
        proj, gates_t = _in_proj(h, mix_norm[layer], w_in[layer])
        ret = _retention(proj, log_gamma, cos2, sin2, ret_head_norm[layer], batch, seq)
        mh = _mlstm(proj, gates_t, conv_w[layer], conv_b[layer], igate_b[layer], fgate_b[layer],
                    mlstm_head_norm[layer], batch, seq)
        h = _out_proj(h, ret, mh, w_out[layer])
        h = _ffn(h, ffn2_norm[layer], ffn2_w_gate[layer], ffn2_w_up[layer], ffn2_w_down[layer], final_norm,
                 final_norm=last)
    if depth == 0:
        raise ValueError("depth must be >= 1")
    return h.reshape(batch, seq, d)
```

```python
import functools

import jax
import jax.numpy as jnp
from jax import lax
from jax.experimental import pallas as pl
from jax.experimental.pallas import tpu as pltpu

F32 = jnp.float32
BF16 = jnp.bfloat16

RET_HEADS = 8
RET_QK_DIM = 128
RET_V_DIM = 256
MLSTM_HEADS = 4
MLSTM_QK_DIM = 256
MLSTM_V_DIM = 512
CONV_WIDTH = 4
CHUNK = 128
ROPE_BASE = 10000.0
NORM_EPS = 1e-6
FFN_RES_WEIGHT = 0.5

RET_QK_W = RET_HEADS * RET_QK_DIM
RET_V_W = RET_HEADS * RET_V_DIM
MLSTM_QK_W = MLSTM_HEADS * MLSTM_QK_DIM
MLSTM_V_W = MLSTM_HEADS * MLSTM_V_DIM
PROJ_W = 2 * RET_QK_W + 2 * RET_V_W + 2 * MLSTM_QK_W + 2 * MLSTM_V_W
N_GATES = 2 * MLSTM_HEADS

LANES = 128
SUBLANES = 8
V7X_VMEM_LIMIT_BYTES = 58 * 1024 * 1024

AUG_W = LANES


def _pick_tile(total, preferred, quantum):
    if total <= preferred:
        return total
    t = (preferred // quantum) * quantum
    while t >= quantum:
        if total % t == 0:
            return t
        t -= quantum
    return total


def _rms_scale(x):
    return lax.rsqrt(jnp.mean(x * x, axis=-1, keepdims=True) + NORM_EPS)


FFN_NORM_ROWS = 128


def _ffn_kernel(x_hbm, g_ref, wg_ref, wu_ref, wd_ref, fg_ref, o_hbm, acc_ref, xn_ref, sem, *, tm, tf, final_norm):
    i = pl.program_id(0)
    f = pl.program_id(1)
    n_tiles = pl.num_programs(0)
    last_f = pl.num_programs(1) - 1

    def x_copy(tile):
        return pltpu.make_async_copy(x_hbm.at[pl.ds(tile * tm, tm)], acc_ref, sem.at[0])

    def y_copy(tile):
        return pltpu.make_async_copy(acc_ref, o_hbm.at[pl.ds(tile * tm, tm)], sem.at[1])

    @pl.when(f == 0)
    def _():
        @pl.when(i > 0)
        def _():
            y_copy(i - 1).wait()

        x_copy(i).start()
        x_copy(i).wait()

        def norm_rows(r, carry):
            rows = pl.ds(pl.multiple_of(r * FFN_NORM_ROWS, FFN_NORM_ROWS), FFN_NORM_ROWS)
            x = acc_ref[rows, :]
            xn_ref[rows, :] = (x * _rms_scale(x) * g_ref[...]).astype(BF16)
            return carry

        lax.fori_loop(0, tm // FFN_NORM_ROWS, norm_rows, 0)

    wgu = jnp.concatenate([wg_ref[...].astype(BF16), wu_ref[...].astype(BF16)], axis=1)
    gu = jnp.dot(xn_ref[...], wgu, preferred_element_type=F32)
    g = gu[:, :tf]
    u = gu[:, tf:]
    h = (g * jax.nn.sigmoid(g) * u * FFN_RES_WEIGHT).astype(BF16)
    acc_ref[...] += jnp.dot(h, wd_ref[...].astype(BF16), preferred_element_type=F32)

    @pl.when(f == last_f)
    def _():
        if final_norm:
            def final_rows(r, carry):
                rows = pl.ds(pl.multiple_of(r * FFN_NORM_ROWS, FFN_NORM_ROWS), FFN_NORM_ROWS)
                y = acc_ref[rows, :]
                acc_ref[rows, :] = y * _rms_scale(y) * fg_ref[...]
                return carry

            lax.fori_loop(0, tm // FFN_NORM_ROWS, final_rows, 0)

        y_copy(i).start()

        @pl.when(i == n_tiles - 1)
        def _():
            y_copy(i).wait()


def _ffn(x2d, norm_g, w_gate, w_up, w_down, final_g, *, final_norm):
    m, d = x2d.shape
    dff = w_gate.shape[1]
    tm = _pick_tile(m, 1024, FFN_NORM_ROWS)
    tf = _pick_tile(dff, 256, LANES)
    assert tm % FFN_NORM_ROWS == 0
    hbm = pl.BlockSpec(memory_space=pl.ANY)
    return pl.pallas_call(
        functools.partial(_ffn_kernel, tm=tm, tf=tf, final_norm=final_norm),
        out_shape=jax.ShapeDtypeStruct((m, d), F32),
        grid=(m // tm, dff // tf),
        in_specs=[
            hbm,
            pl.BlockSpec((1, d), lambda i, f: (0, 0)),
            pl.BlockSpec((d, tf), lambda i, f: (0, f)),
            pl.BlockSpec((d, tf), lambda i, f: (0, f)),
            pl.BlockSpec((tf, d), lambda i, f: (f, 0)),
            pl.BlockSpec((1, d), lambda i, f: (0, 0)),
        ],
        out_specs=hbm,
        scratch_shapes=[
            pltpu.VMEM((tm, d), F32),
            pltpu.VMEM((tm, d), BF16),
            pltpu.SemaphoreType.DMA((2,)),
        ],
        compiler_params=pltpu.CompilerParams(
            dimension_semantics=("arbitrary", "arbitrary"), vmem_limit_bytes=V7X_VMEM_LIMIT_BYTES
        ),
        name="ffn_final" if final_norm else "ffn",
    )(x2d, norm_g.reshape(1, d), w_gate, w_up, w_down, final_g.reshape(1, d))


def _inproj_kernel(x_ref, g_ref, wt_ref, wgt_ref, p_ref, gt_ref, xn_ref):
    n = pl.program_id(1)
    trans_b = (((1,), (1,)), ((), ()))

    @pl.when(n == 0)
    def _():
        x = x_ref[...]
        xn = (x * _rms_scale(x) * g_ref[...]).astype(BF16)
        xn_ref[...] = xn
        wg = jnp.concatenate([wgt_ref[...], jnp.zeros_like(wgt_ref)], axis=0).astype(BF16)
        gt = lax.dot_general(wg, xn, trans_b, preferred_element_type=F32)
        gt_ref[...] = gt[:SUBLANES, :]

    p_ref[...] = lax.dot_general(xn_ref[...], wt_ref[...], trans_b, preferred_element_type=F32).astype(BF16)


def _in_proj(x2d, norm_g, w_in):
    m, d = x2d.shape
    assert N_GATES == SUBLANES and PROJ_W % SUBLANES == 0
    tm = _pick_tile(m, 512, LANES)
    tn = _pick_tile(PROJ_W, 1024, LANES)
    w_t = w_in.T
    return pl.pallas_call(
        _inproj_kernel,
        out_shape=(
            jax.ShapeDtypeStruct((m, PROJ_W), BF16),
            jax.ShapeDtypeStruct((SUBLANES, m), F32),
        ),
        grid=(m // tm, PROJ_W // tn),
        in_specs=[
            pl.BlockSpec((tm, d), lambda i, n: (i, 0)),
            pl.BlockSpec((1, d), lambda i, n: (0, 0)),
            pl.BlockSpec((tn, d), lambda i, n: (n, 0)),
            pl.BlockSpec((N_GATES, d), lambda i, n: (PROJ_W // N_GATES, 0)),
        ],
        out_specs=(
            pl.BlockSpec((tm, tn), lambda i, n: (i, n)),
            pl.BlockSpec((SUBLANES, tm), lambda i, n: (0, i)),
        ),
        scratch_shapes=[pltpu.VMEM((tm, d), BF16)],
        compiler_params=pltpu.CompilerParams(
            dimension_semantics=("parallel", "arbitrary"), vmem_limit_bytes=V7X_VMEM_LIMIT_BYTES
        ),
        name="in_proj",
    )(x2d, norm_g.reshape(1, d), w_t.astype(BF16), w_t)


def _retention_kernel(lg_ref, q_ref, k_ref, v_ref, g_ref, cos_ref, sin_ref, hn_ref, o_ref, state_ref, *, n_chunks):
    h = pl.program_id(1)
    c = pl.program_id(2)

    @pl.when(c == 0)
    def _():
        state_ref[...] = jnp.zeros_like(state_ref)

    lg = lg_ref[h]
    row = lax.broadcasted_iota(jnp.int32, (CHUNK, CHUNK), 0)
    col = lax.broadcasted_iota(jnp.int32, (CHUNK, CHUNK), 1)
    rel = (row - col).astype(F32)
    intra = jnp.where(rel >= 0, jnp.exp(lg * rel), 0.0)
    pos = lax.broadcasted_iota(jnp.int32, (CHUNK, 1), 0).astype(F32)
    q_decay = jnp.exp(lg * (pos + 1.0))
    k_decay = jnp.exp(lg * (CHUNK - 1.0 - pos))
    chunk_decay = jnp.exp(lg * CHUNK)
    k_scale = RET_QK_DIM ** -0.5
    hn = hn_ref[...]

    for ci in range(n_chunks):
        rows = pl.ds(ci * CHUNK, CHUNK)
        cos = cos_ref[rows, :]
        sin = sin_ref[rows, :]
        q = q_ref[rows, :].astype(F32)
        k = k_ref[rows, :].astype(F32)
        q = q * cos + pltpu.roll(q, RET_QK_DIM // 2, 1) * sin
        k = (k * cos + pltpu.roll(k, RET_QK_DIM // 2, 1) * sin) * k_scale
        qb = q.astype(BF16)
        v = v_ref[rows, :]
        state = state_ref[...]
        scores = lax.dot_general(qb, k.astype(BF16), (((1,), (1,)), ((), ())), preferred_element_type=F32) * intra
        out = jnp.dot(scores.astype(BF16), v, preferred_element_type=F32)
        out = out + jnp.dot(qb, state.astype(BF16), preferred_element_type=F32) * q_decay
        kd_t = (k * k_decay).T.astype(BF16)
        state_ref[...] = state * chunk_decay + jnp.dot(kd_t, v, preferred_element_type=F32)
        gate = g_ref[rows, :].astype(F32)
        y = out * _rms_scale(out) * hn * (gate * jax.nn.sigmoid(gate))
        o_ref[rows, :] = y.astype(BF16)


def _retention(proj, log_gamma, cos2, sin2, head_norm, batch, seq):
    m = proj.shape[0]
    tr = _pick_tile(seq, 512, CHUNK)
    tiles = seq // tr
    q_blk, k_blk = 0, RET_QK_W // RET_QK_DIM
    v_blk = (2 * RET_QK_W) // RET_V_DIM
    g_blk = (2 * RET_QK_W + RET_V_W) // RET_V_DIM
    return pl.pallas_call(
        functools.partial(_retention_kernel, n_chunks=tr // CHUNK),
        out_shape=jax.ShapeDtypeStruct((m, RET_V_W), BF16),
        grid=(batch, RET_HEADS, tiles),
        in_specs=[
            pl.BlockSpec(memory_space=pltpu.SMEM),
            pl.BlockSpec((tr, RET_QK_DIM), lambda b, h, c: (b * tiles + c, q_blk + h)),
            pl.BlockSpec((tr, RET_QK_DIM), lambda b, h, c: (b * tiles + c, k_blk + h)),
            pl.BlockSpec((tr, RET_V_DIM), lambda b, h, c: (b * tiles + c, v_blk + h)),
            pl.BlockSpec((tr, RET_V_DIM), lambda b, h, c: (b * tiles + c, g_blk + h)),
            pl.BlockSpec((tr, RET_QK_DIM), lambda b, h, c: (c, 0)),
            pl.BlockSpec((tr, RET_QK_DIM), lambda b, h, c: (c, 0)),
            pl.BlockSpec((1, RET_V_DIM), lambda b, h, c: (0, h)),
        ],
        out_specs=pl.BlockSpec((tr, RET_V_DIM), lambda b, h, c: (b * tiles + c, h)),
        scratch_shapes=[pltpu.VMEM((RET_QK_DIM, RET_V_DIM), F32)],
        compiler_params=pltpu.CompilerParams(
            dimension_semantics=("parallel", "parallel", "arbitrary"), vmem_limit_bytes=V7X_VMEM_LIMIT_BYTES
        ),
        name="retention",
    )(log_gamma, proj, proj, proj, proj, cos2, sin2, head_norm.reshape(1, RET_V_W))


def _log_sigmoid(x):
    return jnp.minimum(x, 0.0) - jnp.log1p(jnp.exp(-jnp.abs(x)))


def _mlstm_kernel(ib_ref, fb_ref, q_ref, k_ref, v_ref, og_ref, gt_ref, cwq_ref, cwk_ref, cbq_ref, cbk_ref, hn_ref,
                  o_ref, c_ref, m_ref, qraw_ref, kraw_ref, qs_ref, ks_ref, *, n_chunks):
    h = pl.program_id(1)
    c = pl.program_id(2)
    tr = n_chunks * CHUNK
    hist = SUBLANES

    @pl.when(c == 0)
    def _():
        c_ref[...] = jnp.zeros_like(c_ref)
        m_ref[...] = jnp.zeros_like(m_ref)
        qraw_ref[pl.ds(0, hist), :] = jnp.zeros((hist, MLSTM_QK_DIM), F32)
        kraw_ref[pl.ds(0, hist), :] = jnp.zeros((hist, MLSTM_QK_DIM), F32)

    def conv_silu(raw_ref, src_ref, w_ref, b_ref):
        raw_ref[pl.ds(hist, tr), :] = src_ref[...].astype(F32)
        acc = jnp.broadcast_to(b_ref[...], (tr, MLSTM_QK_DIM))
        for j in range(CONV_WIDTH):
            acc = acc + raw_ref[pl.ds(hist - (CONV_WIDTH - 1) + j, tr), :] * w_ref[pl.ds(j, 1), :]
        raw_ref[pl.ds(0, hist), :] = raw_ref[pl.ds(tr, hist), :]
        return acc * jax.nn.sigmoid(acc)

    qs_ref[...] = conv_silu(qraw_ref, q_ref, cwq_ref, cbq_ref).astype(BF16)
    ks_ref[...] = (conv_silu(kraw_ref, k_ref, cwk_ref, cbk_ref) * (MLSTM_QK_DIM ** -0.5)).astype(BF16)

    ib = ib_ref[h]
    fb = fb_ref[h]
    hn = hn_ref[...]
    row = lax.broadcasted_iota(jnp.int32, (CHUNK, CHUNK), 0)
    col = lax.broadcasted_iota(jnp.int32, (CHUNK, CHUNK), 1)
    causal = col <= row
    lane = lax.broadcasted_iota(jnp.int32, (1, CHUNK), 1)
    gate_row = lax.broadcasted_iota(jnp.int32, (SUBLANES, CHUNK), 0)
    ones_col = (lax.broadcasted_iota(jnp.int32, (CHUNK, AUG_W), 1) == 0).astype(BF16)

    for ci in range(n_chunks):
        rows = pl.ds(ci * CHUNK, CHUNK)
        gates = gt_ref[:, rows]
        i_row = jnp.sum(jnp.where(gate_row == h, gates, 0.0), axis=0, keepdims=True) + ib
        f_row = jnp.sum(jnp.where(gate_row == MLSTM_HEADS + h, gates, 0.0), axis=0, keepdims=True) + fb
        lf_row = _log_sigmoid(f_row)
        b_row = lf_row
        shift = 1
        while shift < CHUNK:
            b_row = b_row + jnp.where(lane >= shift, pltpu.roll(b_row, shift, 1), 0.0)
            shift *= 2
        stacked = jnp.where(row == 0, b_row, jnp.where(row == 1, i_row, 0.0))
        stacked_t = stacked.T
        b_col = stacked_t[:, 0:1]
        i_col = stacked_t[:, 1:2]

        m_prev = m_ref[:, 0:1]
        d_log = jnp.where(causal, b_col - b_row + i_row, -jnp.inf)
        inter = b_col + m_prev
        m_t = jnp.maximum(inter, jnp.max(d_log, axis=1, keepdims=True))
        w_inter = jnp.exp(inter - m_t)
        w_intra = jnp.exp(d_log - m_t)

        q = qs_ref[rows, :]
        k = ks_ref[rows, :]
        v_aug = jnp.concatenate([v_ref[rows, :], ones_col], axis=1)
        state = c_ref[...]
        s = lax.dot_general(q, k, (((1,), (1,)), ((), ())), preferred_element_type=F32) * w_intra
        tot = jnp.dot(s.astype(BF16), v_aug, preferred_element_type=F32)
        tot = tot + w_inter * jnp.dot(q, state.astype(BF16), preferred_element_type=F32)
        num = tot[:, :MLSTM_V_DIM]
        nq = tot[:, MLSTM_V_DIM:MLSTM_V_DIM + 1]
        hid = num / jnp.maximum(jnp.abs(nq), jnp.exp(-m_t))

        b_last = b_row[:, CHUNK - 1:CHUNK]
        m_new = m_t[CHUNK - 1:CHUNK, :]
        w_state = jnp.exp(b_last + m_prev - m_new)
        w_key = jnp.exp(b_last - b_col + i_col - m_new)
        kw_t = (k.astype(F32) * w_key).T.astype(BF16)
        c_ref[...] = w_state * state + jnp.dot(kw_t, v_aug, preferred_element_type=F32)
        m_ref[...] = jnp.broadcast_to(m_new, m_ref.shape)

        og = og_ref[rows, :].astype(F32)
        y = hid * _rms_scale(hid) * hn * jax.nn.sigmoid(og)
        o_ref[rows, :] = y.astype(BF16)


def _mlstm(proj, gates_t, conv_w, conv_b, igate_b, fgate_b, head_norm, batch, seq):
    m = proj.shape[0]
    tr = _pick_tile(seq, 512, CHUNK)
    tiles = seq // tr
    base = 2 * RET_QK_W + 2 * RET_V_W
    q_blk = base // MLSTM_QK_DIM
    k_blk = (base + MLSTM_QK_W) // MLSTM_QK_DIM
    v_blk = (base + 2 * MLSTM_QK_W) // MLSTM_V_DIM
    o_blk = (base + 2 * MLSTM_QK_W + MLSTM_V_W) // MLSTM_V_DIM
    smem = pl.BlockSpec(memory_space=pltpu.SMEM)
    return pl.pallas_call(
        functools.partial(_mlstm_kernel, n_chunks=tr // CHUNK),
        out_shape=jax.ShapeDtypeStruct((m, MLSTM_V_W), BF16),
        grid=(batch, MLSTM_HEADS, tiles),
        in_specs=[
            smem,
            smem,
            pl.BlockSpec((tr, MLSTM_QK_DIM), lambda b, h, c: (b * tiles + c, q_blk + h)),
            pl.BlockSpec((tr, MLSTM_QK_DIM), lambda b, h, c: (b * tiles + c, k_blk + h)),
            pl.BlockSpec((tr, MLSTM_V_DIM), lambda b, h, c: (b * tiles + c, v_blk + h)),
            pl.BlockSpec((tr, MLSTM_V_DIM), lambda b, h, c: (b * tiles + c, o_blk + h)),
            pl.BlockSpec((SUBLANES, tr), lambda b, h, c: (0, b * tiles + c)),
            pl.BlockSpec((CONV_WIDTH, MLSTM_QK_DIM), lambda b, h, c: (0, h)),
            pl.BlockSpec((CONV_WIDTH, MLSTM_QK_DIM), lambda b, h, c: (0, MLSTM_HEADS + h)),
            pl.BlockSpec((1, MLSTM_QK_DIM), lambda b, h, c: (0, h)),
            pl.BlockSpec((1, MLSTM_QK_DIM), lambda b, h, c: (0, MLSTM_HEADS + h)),
            pl.BlockSpec((1, MLSTM_V_DIM), lambda b, h, c: (0, h)),
        ],
        out_specs=pl.BlockSpec((tr, MLSTM_V_DIM), lambda b, h, c: (b * tiles + c, h)),
        scratch_shapes=[
            pltpu.VMEM((MLSTM_QK_DIM, MLSTM_V_DIM + AUG_W), F32),
            pltpu.VMEM((1, LANES), F32),
            pltpu.VMEM((tr + SUBLANES, MLSTM_QK_DIM), F32),
            pltpu.VMEM((tr + SUBLANES, MLSTM_QK_DIM), F32),
            pltpu.VMEM((tr, MLSTM_QK_DIM), BF16),
            pltpu.VMEM((tr, MLSTM_QK_DIM), BF16),
        ],
        compiler_params=pltpu.CompilerParams(
            dimension_semantics=("parallel", "parallel", "arbitrary"), vmem_limit_bytes=V7X_VMEM_LIMIT_BYTES
        ),
        name="mlstm",
    )(igate_b, fgate_b, proj, proj, proj, proj, gates_t, conv_w, conv_w, conv_b.reshape(1, -1),
      conv_b.reshape(1, -1), head_norm.reshape(1, MLSTM_V_W))


def _outproj_kernel(x_ref, r_ref, h_ref, wr_ref, wh_ref, o_ref):
    acc = jnp.dot(r_ref[...], wr_ref[...], preferred_element_type=F32)
    acc = acc + jnp.dot(h_ref[...], wh_ref[...], preferred_element_type=F32)
    o_ref[...] = x_ref[...] + acc


def _out_proj(x2d, ret, mh, w_out):
    m, d = x2d.shape
    tm = _pick_tile(m, 512, SUBLANES)
    tn = _pick_tile(d, 1024, LANES)
    wo = w_out.astype(BF16)
    return pl.pallas_call(
        _outproj_kernel,
        out_shape=jax.ShapeDtypeStruct((m, d), F32),
        grid=(m // tm, d // tn),
        in_specs=[
            pl.BlockSpec((tm, tn), lambda i, n: (i, n)),
            pl.BlockSpec((tm, RET_V_W), lambda i, n: (i, 0)),
            pl.BlockSpec((tm, MLSTM_V_W), lambda i, n: (i, 0)),
            pl.BlockSpec((RET_V_W, tn), lambda i, n: (0, n)),
            pl.BlockSpec((MLSTM_V_W, tn), lambda i, n: (1, n)),
        ],
        out_specs=pl.BlockSpec((tm, tn), lambda i, n: (i, n)),
        compiler_params=pltpu.CompilerParams(
            dimension_semantics=("parallel", "parallel"), vmem_limit_bytes=V7X_VMEM_LIMIT_BYTES
        ),
        name="out_proj",
    )(x2d, ret, mh, wo, wo)


def _rotary_tables(seq):
    half = RET_QK_DIM // 2
    inv_freq = ROPE_BASE ** (-jnp.arange(half, dtype=F32) / half)
    ang = jnp.arange(seq, dtype=F32)[:, None] * inv_freq[None, :]
    cos, sin = jnp.cos(ang), jnp.sin(ang)
    return jnp.concatenate([cos, cos], axis=1), jnp.concatenate([-sin, sin], axis=1)


def kernel(x, ffn1_norm, ffn1_w_gate, ffn1_w_up, ffn1_w_down, mix_norm, w_in, conv_w, conv_b, igate_b, fgate_b,
           ret_head_norm, mlstm_head_norm, w_out, ffn2_norm, ffn2_w_gate, ffn2_w_up, ffn2_w_down, final_norm):
    batch, seq, d = x.shape
    depth = ffn1_norm.shape[0]
    assert seq % CHUNK == 0 and w_in.shape[-1] == PROJ_W + N_GATES and w_out.shape[1] == RET_V_W + MLSTM_V_W
    log_gamma = jnp.log1p(-jnp.exp2(-5.0 - jnp.arange(RET_HEADS, dtype=F32)))
    cos2, sin2 = _rotary_tables(seq)
    h = x.reshape(batch * seq, d)
    for layer in range(depth):
        last = layer == depth - 1
        h = _ffn(h, ffn1_norm[layer], ffn1_w_gate[layer], ffn1_w_up[layer], ffn1_w_down[layer], final_norm,
                 final_norm=False)
        proj, gates_t = _in_proj(h, mix_norm[layer], w_in[layer])
        ret = _retention(proj, log_gamma, cos2, sin2, ret_head_norm[layer], batch, seq)
        mh = _mlstm(proj, gates_t, conv_w[layer], conv_b[layer], igate_b[layer], fgate_b[layer],
                    mlstm_head_norm[layer], batch, seq)
        h = _out_proj(h, ret, mh, w_out[layer])
        h = _ffn(h, ffn2_norm[layer], ffn2_w_gate[layer], ffn2_w_up[layer], ffn2_w_down[layer], final_norm,
                 final_norm=last)
    return h.reshape(batch, seq, d)
```

```python
import functools

import jax
import jax.numpy as jnp
from jax import lax
from jax.experimental import pallas as pl
from jax.experimental.pallas import tpu as pltpu

F32 = jnp.float32
BF16 = jnp.bfloat16

RET_HEADS = 8
RET_QK_DIM = 128
RET_V_DIM = 256
MLSTM_HEADS = 4
MLSTM_QK_DIM = 256
MLSTM_V_DIM = 512
CONV_WIDTH = 4
CHUNK = 128
ROPE_BASE = 10000.0
NORM_EPS = 1e-6
FFN_RES_WEIGHT = 0.5

RET_QK_W = RET_HEADS * RET_QK_DIM
RET_V_W = RET_HEADS * RET_V_DIM
MLSTM_QK_W = MLSTM_HEADS * MLSTM_QK_DIM
MLSTM_V_W = MLSTM_HEADS * MLSTM_V_DIM
PROJ_W = 2 * RET_QK_W + 2 * RET_V_W + 2 * MLSTM_QK_W + 2 * MLSTM_V_W
N_GATES = 2 * MLSTM_HEADS

LANES = 128
SUBLANES = 8
V7X_VMEM_LIMIT_BYTES = 58 * 1024 * 1024

AUG_W = LANES


def _pick_tile(total, preferred, quantum):
    if total <= preferred:
        return total
    t = (preferred // quantum) * quantum
    while t >= quantum:
        if total % t == 0:
            return t
        t -= quantum
    return total


def _rms_scale(x):
    return lax.rsqrt(jnp.mean(x * x, axis=-1, keepdims=True) + NORM_EPS)


FFN_NORM_ROWS = 128


def _ffn_kernel(x_hbm, g_ref, wg_ref, wu_ref, wd_ref, fg_ref, o_hbm, acc_ref, xn_ref, sem, *, tm, tf, final_norm):
    i = pl.program_id(0)
    f = pl.program_id(1)
    n_tiles = pl.num_programs(0)
    last_f = pl.num_programs(1) - 1

    def x_copy(tile):
        return pltpu.make_async_copy(x_hbm.at[pl.ds(tile * tm, tm)], acc_ref, sem.at[0])

    def y_copy(tile):
        return pltpu.make_async_copy(acc_ref, o_hbm.at[pl.ds(tile * tm, tm)], sem.at[1])

    @pl.when(f == 0)
    def _():
        @pl.when(i > 0)
        def _():
            y_copy(i - 1).wait()

        x_copy(i).start()
        x_copy(i).wait()

        def norm_rows(r, carry):
            rows = pl.ds(pl.multiple_of(r * FFN_NORM_ROWS, FFN_NORM_ROWS), FFN_NORM_ROWS)
            x = acc_ref[rows, :]
            xn_ref[rows, :] = (x * _rms_scale(x) * g_ref[...]).astype(BF16)
            return carry

        lax.fori_loop(0, tm // FFN_NORM_ROWS, norm_rows, 0)

    wgu = jnp.concatenate([wg_ref[...].astype(BF16), wu_ref[...].astype(BF16)], axis=1)
    gu = jnp.dot(xn_ref[...], wgu, preferred_element_type=F32)
    g = gu[:, :tf]
    u = gu[:, tf:]
    h = (g * jax.nn.sigmoid(g) * u * FFN_RES_WEIGHT).astype(BF16)
    acc_ref[...] += jnp.dot(h, wd_ref[...].astype(BF16), preferred_element_type=F32)

    @pl.when(f == last_f)
    def _():
        if final_norm:
            def final_rows(r, carry):
                rows = pl.ds(pl.multiple_of(r * FFN_NORM_ROWS, FFN_NORM_ROWS), FFN_NORM_ROWS)
                y = acc_ref[rows, :]
                acc_ref[rows, :] = y * _rms_scale(y) * fg_ref[...]
                return carry

            lax.fori_loop(0, tm // FFN_NORM_ROWS, final_rows, 0)

        y_copy(i).start()

        @pl.when(i == n_tiles - 1)
        def _():
            y_copy(i).wait()


def _ffn(x2d, norm_g, w_gate, w_up, w_down, final_g, *, final_norm):
    m, d = x2d.shape
    dff = w_gate.shape[1]
    tm = _pick_tile(m, 1024, FFN_NORM_ROWS)
    tf = _pick_tile(dff, 256, LANES)
    assert tm % FFN_NORM_ROWS == 0
    hbm = pl.BlockSpec(memory_space=pl.ANY)
    return pl.pallas_call(
        functools.partial(_ffn_kernel, tm=tm, tf=tf, final_norm=final_norm),
        out_shape=jax.ShapeDtypeStruct((m, d), F32),
        grid=(m // tm, dff // tf),
        in_specs=[
            hbm,
            pl.BlockSpec((1, d), lambda i, f: (0, 0)),
            pl.BlockSpec((d, tf), lambda i, f: (0, f)),
            pl.BlockSpec((d, tf), lambda i, f: (0, f)),
            pl.BlockSpec((tf, d), lambda i, f: (f, 0)),
            pl.BlockSpec((1, d), lambda i, f: (0, 0)),
        ],
        out_specs=hbm,
        scratch_shapes=[
            pltpu.VMEM((tm, d), F32),
            pltpu.VMEM((tm, d), BF16),
            pltpu.SemaphoreType.DMA((2,)),
        ],
        compiler_params=pltpu.CompilerParams(
            dimension_semantics=("arbitrary", "arbitrary"), vmem_limit_bytes=V7X_VMEM_LIMIT_BYTES
        ),
        name="ffn_final" if final_norm else "ffn",
    )(x2d, norm_g.reshape(1, d), w_gate, w_up, w_down, final_g.reshape(1, d))


def _inproj_kernel(x_ref, g_ref, wt_ref, wgt_ref, p_ref, gt_ref, xn_ref):
    n = pl.program_id(1)
    trans_b = (((1,), (1,)), ((), ()))

    @pl.when(n == 0)
    def _():
        x = x_ref[...]
        xn = (x * _rms_scale(x) * g_ref[...]).astype(BF16)
        xn_ref[...] = xn
        wg = jnp.concatenate([wgt_ref[...], jnp.zeros_like(wgt_ref)], axis=0).astype(BF16)
        gt = lax.dot_general(wg, xn, trans_b, preferred_element_type=F32)
        gt_ref[...] = gt[:SUBLANES, :]

    p_ref[...] = lax.dot_general(xn_ref[...], wt_ref[...], trans_b, preferred_element_type=F32).astype(BF16)


def _in_proj(x2d, norm_g, w_in):
    m, d = x2d.shape
    assert N_GATES == SUBLANES and PROJ_W % SUBLANES == 0
    tm = _pick_tile(m, 512, LANES)
    tn = _pick_tile(PROJ_W, 1024, LANES)
    w_t = w_in.T
    return pl.pallas_call(
        _inproj_kernel,
        out_shape=(
            jax.ShapeDtypeStruct((m, PROJ_W), BF16),
            jax.ShapeDtypeStruct((SUBLANES, m), F32),
        ),
        grid=(m // tm, PROJ_W // tn),
        in_specs=[
            pl.BlockSpec((tm, d), lambda i, n: (i, 0)),
            pl.BlockSpec((1, d), lambda i, n: (0, 0)),
            pl.BlockSpec((tn, d), lambda i, n: (n, 0)),
            pl.BlockSpec((N_GATES, d), lambda i, n: (PROJ_W // N_GATES, 0)),
        ],
        out_specs=(
            pl.BlockSpec((tm, tn), lambda i, n: (i, n)),
            pl.BlockSpec((SUBLANES, tm), lambda i, n: (0, i)),
        ),
        scratch_shapes=[pltpu.VMEM((tm, d), BF16)],
        compiler_params=pltpu.CompilerParams(
            dimension_semantics=("parallel", "arbitrary"), vmem_limit_bytes=V7X_VMEM_LIMIT_BYTES
        ),
        name="in_proj",
    )(x2d, norm_g.reshape(1, d), w_t.astype(BF16), w_t)


def _retention_kernel(lg_ref, q_ref, k_ref, v_ref, g_ref, cos_ref, sin_ref, hn_ref, o_ref, state_ref, *, n_chunks):
    c = pl.program_id(1)

    @pl.when(c == 0)
    def _():
        state_ref[...] = jnp.zeros_like(state_ref)

    row = lax.broadcasted_iota(jnp.int32, (CHUNK, CHUNK), 0)
    col = lax.broadcasted_iota(jnp.int32, (CHUNK, CHUNK), 1)
    rel = (row - col).astype(F32)
    pos = lax.broadcasted_iota(jnp.int32, (CHUNK, 1), 0).astype(F32)
    k_scale = RET_QK_DIM ** -0.5
    trans_b = (((1,), (1,)), ((), ()))

    decays = []
    for h in range(RET_HEADS):
        lg = lg_ref[h]
        decays.append((
            jnp.where(rel >= 0, jnp.exp(lg * rel), 0.0),
            jnp.exp(lg * (pos + 1.0)),
            jnp.exp(lg * (CHUNK - 1.0 - pos)),
            jnp.exp(lg * CHUNK),
        ))

    for ci in range(n_chunks):
        rows = pl.ds(ci * CHUNK, CHUNK)
        cos = cos_ref[rows, :]
        sin = sin_ref[rows, :]
        for h in range(RET_HEADS):
            intra, q_decay, k_decay, chunk_decay = decays[h]
            qk_cols = pl.ds(h * RET_QK_DIM, RET_QK_DIM)
            v_cols = pl.ds(h * RET_V_DIM, RET_V_DIM)
            q = q_ref[rows, qk_cols].astype(F32)
            k = k_ref[rows, qk_cols].astype(F32)
            q = q * cos + pltpu.roll(q, RET_QK_DIM // 2, 1) * sin
            k = (k * cos + pltpu.roll(k, RET_QK_DIM // 2, 1) * sin) * k_scale
            qb = q.astype(BF16)
            v = v_ref[rows, v_cols]
            state = state_ref[h]
            scores = lax.dot_general(qb, k.astype(BF16), trans_b, preferred_element_type=F32) * intra
            out = jnp.dot(scores.astype(BF16), v, preferred_element_type=F32)
            out = out + jnp.dot(qb, state.astype(BF16), preferred_element_type=F32) * q_decay
            kd_t = (k * k_decay).T.astype(BF16)
            state_ref[h] = state * chunk_decay + jnp.dot(kd_t, v, preferred_element_type=F32)
            gate = g_ref[rows, v_cols].astype(F32)
            y = out * _rms_scale(out) * hn_ref[:, v_cols] * (gate * jax.nn.sigmoid(gate))
            o_ref[rows, v_cols] = y.astype(BF16)


def _retention(proj, log_gamma, cos2, sin2, head_norm, batch, seq):
    m = proj.shape[0]
    tr = _pick_tile(seq, 512, CHUNK)
    tiles = seq // tr
    assert RET_V_W == 2 * RET_QK_W
    return pl.pallas_call(
        functools.partial(_retention_kernel, n_chunks=tr // CHUNK),
        out_shape=jax.ShapeDtypeStruct((m, RET_V_W), BF16),
        grid=(batch, tiles),
        in_specs=[
            pl.BlockSpec(memory_space=pltpu.SMEM),
            pl.BlockSpec((tr, RET_QK_W), lambda b, c: (b * tiles + c, 0)),
            pl.BlockSpec((tr, RET_QK_W), lambda b, c: (b * tiles + c, 1)),
            pl.BlockSpec((tr, RET_V_W), lambda b, c: (b * tiles + c, 1)),
            pl.BlockSpec((tr, RET_V_W), lambda b, c: (b * tiles + c, 2)),
            pl.BlockSpec((tr, RET_QK_DIM), lambda b, c: (c, 0)),
            pl.BlockSpec((tr, RET_QK_DIM), lambda b, c: (c, 0)),
            pl.BlockSpec((1, RET_V_W), lambda b, c: (0, 0)),
        ],
        out_specs=pl.BlockSpec((tr, RET_V_W), lambda b, c: (b * tiles + c, 0)),
        scratch_shapes=[pltpu.VMEM((RET_HEADS, RET_QK_DIM, RET_V_DIM), F32)],
        compiler_params=pltpu.CompilerParams(
            dimension_semantics=("parallel", "arbitrary"), vmem_limit_bytes=V7X_VMEM_LIMIT_BYTES
        ),
        name="retention",
    )(log_gamma, proj, proj, proj, proj, cos2, sin2, head_norm.reshape(1, RET_V_W))


def _log_sigmoid(x):
    return jnp.minimum(x, 0.0) - jnp.log1p(jnp.exp(-jnp.abs(x)))


def _mlstm_kernel(ib_ref, fb_ref, q_ref, k_ref, v_ref, og_ref, gt_ref, cwq_ref, cwk_ref, cbq_ref, cbk_ref, hn_ref,
                  o_ref, c_ref, m_ref, qraw_ref, kraw_ref, qs_ref, ks_ref, *, n_chunks):
    c = pl.program_id(1)
    tr = n_chunks * CHUNK
    hist = SUBLANES

    @pl.when(c == 0)
    def _():
        c_ref[...] = jnp.zeros_like(c_ref)
        m_ref[...] = jnp.zeros_like(m_ref)
        qraw_ref[pl.ds(0, hist), :] = jnp.zeros((hist, MLSTM_QK_W), F32)
        kraw_ref[pl.ds(0, hist), :] = jnp.zeros((hist, MLSTM_QK_W), F32)

    def conv_silu(raw_ref, src_ref, w_ref, b_ref, dst_ref, scale):
        raw_ref[pl.ds(hist, tr), :] = src_ref[...].astype(F32)
        for ci in range(n_chunks):
            acc = jnp.broadcast_to(b_ref[...], (CHUNK, MLSTM_QK_W))
            for j in range(CONV_WIDTH):
                taps = raw_ref[pl.ds(ci * CHUNK + hist - (CONV_WIDTH - 1) + j, CHUNK), :]
                acc = acc + taps * w_ref[pl.ds(j, 1), :]
            dst_ref[pl.ds(ci * CHUNK, CHUNK), :] = (acc * jax.nn.sigmoid(acc) * scale).astype(BF16)
        raw_ref[pl.ds(0, hist), :] = raw_ref[pl.ds(tr, hist), :]

    conv_silu(qraw_ref, q_ref, cwq_ref, cbq_ref, qs_ref, 1.0)
    conv_silu(kraw_ref, k_ref, cwk_ref, cbk_ref, ks_ref, MLSTM_QK_DIM ** -0.5)

    row = lax.broadcasted_iota(jnp.int32, (CHUNK, CHUNK), 0)
    col = lax.broadcasted_iota(jnp.int32, (CHUNK, CHUNK), 1)
    causal = col <= row
    lane = lax.broadcasted_iota(jnp.int32, (1, CHUNK), 1)
    ones_col = (lax.broadcasted_iota(jnp.int32, (CHUNK, AUG_W), 1) == 0).astype(BF16)
    trans_b = (((1,), (1,)), ((), ()))

    for ci in range(n_chunks):
        rows = pl.ds(ci * CHUNK, CHUNK)
        gates = gt_ref[:, rows]
        for h in range(MLSTM_HEADS):
            qk_cols = pl.ds(h * MLSTM_QK_DIM, MLSTM_QK_DIM)
            v_cols = pl.ds(h * MLSTM_V_DIM, MLSTM_V_DIM)
            i_row = gates[h:h + 1, :] + ib_ref[h]
            lf_row = _log_sigmoid(gates[MLSTM_HEADS + h:MLSTM_HEADS + h + 1, :] + fb_ref[h])
            b_row = lf_row
            shift = 1
            while shift < CHUNK:
                b_row = b_row + jnp.where(lane >= shift, pltpu.roll(b_row, shift, 1), 0.0)
                shift *= 2
            stacked = jnp.where(row == 0, b_row, jnp.where(row == 1, i_row, 0.0))
            stacked_t = stacked.T
            b_col = stacked_t[:, 0:1]
            i_col = stacked_t[:, 1:2]

            m_prev = m_ref[h, :, 0:1]
            d_log = jnp.where(causal, b_col - b_row + i_row, -jnp.inf)
            inter = b_col + m_prev
            m_t = jnp.maximum(inter, jnp.max(d_log, axis=1, keepdims=True))
            w_inter = jnp.exp(inter - m_t)
            w_intra = jnp.exp(d_log - m_t)

            q = qs_ref[rows, qk_cols]
            k = ks_ref[rows, qk_cols]
            v_aug = jnp.concatenate([v_ref[rows, v_cols], ones_col], axis=1)
            state = c_ref[h]
            s = lax.dot_general(q, k, trans_b, preferred_element_type=F32) * w_intra
            tot = jnp.dot(s.astype(BF16), v_aug, preferred_element_type=F32)
            tot = tot + w_inter * jnp.dot(q, state.astype(BF16), preferred_element_type=F32)
            num = tot[:, :MLSTM_V_DIM]
            nq = tot[:, MLSTM_V_DIM:MLSTM_V_DIM + 1]
            hid = num / jnp.maximum(jnp.abs(nq), jnp.exp(-m_t))

            b_last = b_row[:, CHUNK - 1:CHUNK]
            m_new = m_t[CHUNK - 1:CHUNK, :]
            w_state = jnp.exp(b_last + m_prev - m_new)
            w_key = jnp.exp(b_last - b_col + i_col - m_new)
            kw_t = (k.astype(F32) * w_key).T.astype(BF16)
            c_ref[h] = w_state * state + jnp.dot(kw_t, v_aug, preferred_element_type=F32)
            m_ref[h] = jnp.broadcast_to(m_new, (1, LANES))

            og = og_ref[rows, v_cols].astype(F32)
            y = hid * _rms_scale(hid) * hn_ref[:, v_cols] * jax.nn.sigmoid(og)
            o_ref[rows, v_cols] = y.astype(BF16)


def _mlstm(proj, gates_t, conv_w, conv_b, igate_b, fgate_b, head_norm, batch, seq):
    m = proj.shape[0]
    tr = _pick_tile(seq, 512, CHUNK)
    tiles = seq // tr
    base = 2 * RET_QK_W + 2 * RET_V_W
    assert base % MLSTM_V_W == 0 and MLSTM_V_W == 2 * MLSTM_QK_W
    q_blk = base // MLSTM_QK_W
    v_blk = (base + 2 * MLSTM_QK_W) // MLSTM_V_W
    smem = pl.BlockSpec(memory_space=pltpu.SMEM)
    return pl.pallas_call(
        functools.partial(_mlstm_kernel, n_chunks=tr // CHUNK),
        out_shape=jax.ShapeDtypeStruct((m, MLSTM_V_W), BF16),
        grid=(batch, tiles),
        in_specs=[
            smem,
            smem,
            pl.BlockSpec((tr, MLSTM_QK_W), lambda b, c: (b * tiles + c, q_blk)),
            pl.BlockSpec((tr, MLSTM_QK_W), lambda b, c: (b * tiles + c, q_blk + 1)),
            pl.BlockSpec((tr, MLSTM_V_W), lambda b, c: (b * tiles + c, v_blk)),
            pl.BlockSpec((tr, MLSTM_V_W), lambda b, c: (b * tiles + c, v_blk + 1)),
            pl.BlockSpec((SUBLANES, tr), lambda b, c: (0, b * tiles + c)),
            pl.BlockSpec((CONV_WIDTH, MLSTM_QK_W), lambda b, c: (0, 0)),
            pl.BlockSpec((CONV_WIDTH, MLSTM_QK_W), lambda b, c: (0, 1)),
            pl.BlockSpec((1, MLSTM_QK_W), lambda b, c: (0, 0)),
            pl.BlockSpec((1, MLSTM_QK_W), lambda b, c: (0, 1)),
            pl.BlockSpec((1, MLSTM_V_W), lambda b, c: (0, 0)),
        ],
        out_specs=pl.BlockSpec((tr, MLSTM_V_W), lambda b, c: (b * tiles + c, 0)),
        scratch_shapes=[
            pltpu.VMEM((MLSTM_HEADS, MLSTM_QK_DIM, MLSTM_V_DIM + AUG_W), F32),
            pltpu.VMEM((MLSTM_HEADS, 1, LANES), F32),
            pltpu.VMEM((tr + SUBLANES, MLSTM_QK_W), F32),
            pltpu.VMEM((tr + SUBLANES, MLSTM_QK_W), F32),
            pltpu.VMEM((tr, MLSTM_QK_W), BF16),
            pltpu.VMEM((tr, MLSTM_QK_W), BF16),
        ],
        compiler_params=pltpu.CompilerParams(
            dimension_semantics=("parallel", "arbitrary"), vmem_limit_bytes=V7X_VMEM_LIMIT_BYTES
        ),
        name="mlstm",
    )(igate_b, fgate_b, proj, proj, proj, proj, gates_t, conv_w, conv_w, conv_b.reshape(1, -1),
      conv_b.reshape(1, -1), head_norm.reshape(1, MLSTM_V_W))


def _outproj_kernel(x_ref, r_ref, h_ref, wr_ref, wh_ref, o_ref):
    acc = jnp.dot(r_ref[...], wr_ref[...], preferred_element_type=F32)
    acc = acc + jnp.dot(h_ref[...], wh_ref[...], preferred_element_type=F32)
    o_ref[...] = x_ref[...] + acc


def _out_proj(x2d, ret, mh, w_out):
    m, d = x2d.shape
    tm = _pick_tile(m, 512, SUBLANES)
    tn = _pick_tile(d, 1024, LANES)
    wo = w_out.astype(BF16)
    return pl.pallas_call(
        _outproj_kernel,
        out_shape=jax.ShapeDtypeStruct((m, d), F32),
        grid=(m // tm, d // tn),
        in_specs=[
            pl.BlockSpec((tm, tn), lambda i, n: (i, n)),
            pl.BlockSpec((tm, RET_V_W), lambda i, n: (i, 0)),
            pl.BlockSpec((tm, MLSTM_V_W), lambda i, n: (i, 0)),
            pl.BlockSpec((RET_V_W, tn), lambda i, n: (0, n)),
            pl.BlockSpec((MLSTM_V_W, tn), lambda i, n: (1, n)),
        ],
        out_specs=pl.BlockSpec((tm, tn), lambda i, n: (i, n)),
        compiler_params=pltpu.CompilerParams(
            dimension_semantics=("parallel", "parallel"), vmem_limit_bytes=V7X_VMEM_LIMIT_BYTES
        ),
        name="out_proj",
    )(x2d, ret, mh, wo, wo)


def _rotary_tables(seq):
    half = RET_QK_DIM // 2
    inv_freq = ROPE_BASE ** (-jnp.arange(half, dtype=F32) / half)
    ang = jnp.arange(seq, dtype=F32)[:, None] * inv_freq[None, :]
    cos, sin = jnp.cos(ang), jnp.sin(ang)
    return jnp.concatenate([cos, cos], axis=1), jnp.concatenate([-sin, sin], axis=1)


def kernel(x, ffn1_norm, ffn1_w_gate, ffn1_w_up, ffn1_w_down, mix_norm, w_in, conv_w, conv_b, igate_b, fgate_b,
           ret_head_norm, mlstm_head_norm, w_out, ffn2_norm, ffn2_w_gate, ffn2_w_up, ffn2_w_down, final_norm):
    batch, seq, d = x.shape
    depth = ffn1_norm.shape[0]
    assert seq % CHUNK == 0 and w_in.shape[-1] == PROJ_W + N_GATES and w_out.shape[1] == RET_V_W + MLSTM_V_W
    log_gamma = jnp.log1p(-jnp.exp2(-5.0 - jnp.arange(RET_HEADS, dtype=F32)))
    cos2, sin2 = _rotary_tables(seq)
    h = x.reshape(batch * seq, d)
    for layer in range(depth):
        last = layer == depth - 1
        h = _ffn(h, ffn1_norm[layer], ffn1_w_gate[layer], ffn1_w_up[layer], ffn1_w_down[layer], final_norm,
                 final_norm=False)
        proj, gates_t = _in_proj(h, mix_norm[layer], w_in[layer])
        ret = _retention(proj, log_gamma, cos2, sin2, ret_head_norm[layer], batch, seq)
        mh = _mlstm(proj, gates_t, conv_w[layer], conv_b[layer], igate_b[layer], fgate_b[layer],
                    mlstm_head_norm[layer], batch, seq)
        h = _out_proj(h, ret, mh, w_out[layer])
        h = _ffn(h, ffn2_norm[layer], ffn2_w_gate[layer], ffn2_w_up[layer], ffn2_w_down[layer], final_norm,
                 final_norm=last)
    return h.reshape(batch, seq, d)
```

```python
import functools

import jax
import jax.numpy as jnp
from jax import lax
from jax.experimental import pallas as pl
from jax.experimental.pallas import tpu as pltpu

F32 = jnp.float32
BF16 = jnp.bfloat16

RET_HEADS = 8
RET_QK_DIM = 128
RET_V_DIM = 256
MLSTM_HEADS = 4
MLSTM_QK_DIM = 256
MLSTM_V_DIM = 512
CONV_WIDTH = 4
CHUNK = 128
ROPE_BASE = 10000.0
NORM_EPS = 1e-6
FFN_RES_WEIGHT = 0.5

RET_QK_W = RET_HEADS * RET_QK_DIM
RET_V_W = RET_HEADS * RET_V_DIM
MLSTM_QK_W = MLSTM_HEADS * MLSTM_QK_DIM
MLSTM_V_W = MLSTM_HEADS * MLSTM_V_DIM
PROJ_W = 2 * RET_QK_W + 2 * RET_V_W + 2 * MLSTM_QK_W + 2 * MLSTM_V_W
N_GATES = 2 * MLSTM_HEADS

LANES = 128
SUBLANES = 8
V7X_VMEM_LIMIT_BYTES = 58 * 1024 * 1024

AUG_W = LANES


def _pick_tile(total, preferred, quantum):
    if total <= preferred:
        return total
    t = (preferred // quantum) * quantum
    while t >= quantum:
        if total % t == 0:
            return t
        t -= quantum
    return total


def _rms_scale(x):
    return lax.rsqrt(jnp.mean(x * x, axis=-1, keepdims=True) + NORM_EPS)


FFN_NORM_ROWS = 128


def _ffn_kernel(x_hbm, g_ref, wg_ref, wu_ref, wd_ref, eg_ref, *rest, tm, tf, epilogue):
    if epilogue == "emit":
        o_hbm, n_hbm, acc_ref, xn_ref, sem = rest
    else:
        o_hbm, acc_ref, xn_ref, sem = rest
    i = pl.program_id(0)
    f = pl.program_id(1)
    n_tiles = pl.num_programs(0)
    last_f = pl.num_programs(1) - 1

    def x_copy(tile):
        return pltpu.make_async_copy(x_hbm.at[pl.ds(tile * tm, tm)], acc_ref, sem.at[0])

    def y_copy(tile):
        return pltpu.make_async_copy(acc_ref, o_hbm.at[pl.ds(tile * tm, tm)], sem.at[1])

    def n_copy(tile):
        return pltpu.make_async_copy(xn_ref, n_hbm.at[pl.ds(tile * tm, tm)], sem.at[2])

    def slab(r):
        return pl.ds(pl.multiple_of(r * FFN_NORM_ROWS, FFN_NORM_ROWS), FFN_NORM_ROWS)

    @pl.when(f == 0)
    def _():
        @pl.when(i > 0)
        def _():
            y_copy(i - 1).wait()
            if epilogue == "emit":
                n_copy(i - 1).wait()

        x_copy(i).start()
        x_copy(i).wait()

        def norm_rows(r, carry):
            x = acc_ref[slab(r), :]
            xn_ref[slab(r), :] = (x * _rms_scale(x) * g_ref[...]).astype(BF16)
            return carry

        lax.fori_loop(0, tm // FFN_NORM_ROWS, norm_rows, 0)

    wgu = jnp.concatenate([wg_ref[...].astype(BF16), wu_ref[...].astype(BF16)], axis=1)
    gu = jnp.dot(xn_ref[...], wgu, preferred_element_type=F32)
    g = gu[:, :tf]
    u = gu[:, tf:]
    h = (g * jax.nn.sigmoid(g) * u * FFN_RES_WEIGHT).astype(BF16)
    acc_ref[...] += jnp.dot(h, wd_ref[...].astype(BF16), preferred_element_type=F32)

    @pl.when(f == last_f)
    def _():
        if epilogue == "final":
            def final_rows(r, carry):
                y = acc_ref[slab(r), :]
                acc_ref[slab(r), :] = y * _rms_scale(y) * eg_ref[...]
                return carry

            lax.fori_loop(0, tm // FFN_NORM_ROWS, final_rows, 0)
        elif epilogue == "emit":
            def emit_rows(r, carry):
                y = acc_ref[slab(r), :]
                xn_ref[slab(r), :] = (y * _rms_scale(y) * eg_ref[...]).astype(BF16)
                return carry

            lax.fori_loop(0, tm // FFN_NORM_ROWS, emit_rows, 0)
            n_copy(i).start()

        y_copy(i).start()

        @pl.when(i == n_tiles - 1)
        def _():
            y_copy(i).wait()
            if epilogue == "emit":
                n_copy(i).wait()


def _ffn(x2d, norm_g, w_gate, w_up, w_down, epilogue_g, *, epilogue):
    m, d = x2d.shape
    dff = w_gate.shape[1]
    tm = _pick_tile(m, 1024, FFN_NORM_ROWS)
    tf = _pick_tile(dff, 256, LANES)
    assert tm % FFN_NORM_ROWS == 0 and epilogue in ("none", "final", "emit")
    hbm = pl.BlockSpec(memory_space=pl.ANY)
    y_shape = jax.ShapeDtypeStruct((m, d), F32)
    emit = epilogue == "emit"
    return pl.pallas_call(
        functools.partial(_ffn_kernel, tm=tm, tf=tf, epilogue=epilogue),
        out_shape=(y_shape, jax.ShapeDtypeStruct((m, d), BF16)) if emit else y_shape,
        grid=(m // tm, dff // tf),
        in_specs=[
            hbm,
            pl.BlockSpec((1, d), lambda i, f: (0, 0)),
            pl.BlockSpec((d, tf), lambda i, f: (0, f)),
            pl.BlockSpec((d, tf), lambda i, f: (0, f)),
            pl.BlockSpec((tf, d), lambda i, f: (f, 0)),
            pl.BlockSpec((1, d), lambda i, f: (0, 0)),
        ],
        out_specs=(hbm, hbm) if emit else hbm,
        scratch_shapes=[
            pltpu.VMEM((tm, d), F32),
            pltpu.VMEM((tm, d), BF16),
            pltpu.SemaphoreType.DMA((3,)),
        ],
        compiler_params=pltpu.CompilerParams(
            dimension_semantics=("arbitrary", "arbitrary"), vmem_limit_bytes=V7X_VMEM_LIMIT_BYTES
        ),
        name="ffn_" + epilogue,
    )(x2d, norm_g.reshape(1, d), w_gate, w_up, w_down, epilogue_g.reshape(1, d))


def _inproj_kernel(xn_ref, wt_ref, wgt_ref, p_ref, gt_ref):
    n = pl.program_id(1)
    trans_b = (((1,), (1,)), ((), ()))

    @pl.when(n == 0)
    def _():
        wg = jnp.concatenate([wgt_ref[...], jnp.zeros_like(wgt_ref)], axis=0).astype(BF16)
        gt = lax.dot_general(wg, xn_ref[...], trans_b, preferred_element_type=F32)
        gt_ref[...] = gt[:SUBLANES, :]

    w = wt_ref[...].astype(BF16)
    p_ref[...] = lax.dot_general(xn_ref[...], w, trans_b, preferred_element_type=F32).astype(BF16)


def _in_proj(xn, w_in):
    m, d = xn.shape
    assert N_GATES == SUBLANES and PROJ_W % SUBLANES == 0
    tm = _pick_tile(m, 1024, LANES)
    tn = _pick_tile(PROJ_W, 768, 2 * LANES)
    w_t = w_in.T
    return pl.pallas_call(
        _inproj_kernel,
        out_shape=(
            jax.ShapeDtypeStruct((m, PROJ_W), BF16),
            jax.ShapeDtypeStruct((SUBLANES, m), F32),
        ),
        grid=(m // tm, PROJ_W // tn),
        in_specs=[
            pl.BlockSpec((tm, d), lambda i, n: (i, 0)),
            pl.BlockSpec((tn, d), lambda i, n: (n, 0)),
            pl.BlockSpec((N_GATES, d), lambda i, n: (PROJ_W // N_GATES, 0)),
        ],
        out_specs=(
            pl.BlockSpec((tm, tn), lambda i, n: (i, n)),
            pl.BlockSpec((SUBLANES, tm), lambda i, n: (0, i)),
        ),
        compiler_params=pltpu.CompilerParams(
            dimension_semantics=("parallel", "arbitrary"), vmem_limit_bytes=V7X_VMEM_LIMIT_BYTES
        ),
        name="in_proj",
    )(xn, w_t, w_t)


def _retention_kernel(lg_ref, q_ref, k_ref, v_ref, g_ref, cos_ref, sin_ref, hn_ref, o_ref, state_ref, *, n_chunks):
    c = pl.program_id(1)

    @pl.when(c == 0)
    def _():
        state_ref[...] = jnp.zeros_like(state_ref)

    row = lax.broadcasted_iota(jnp.int32, (CHUNK, CHUNK), 0)
    col = lax.broadcasted_iota(jnp.int32, (CHUNK, CHUNK), 1)
    rel = (row - col).astype(F32)
    pos = lax.broadcasted_iota(jnp.int32, (CHUNK, 1), 0).astype(F32)
    k_scale = RET_QK_DIM ** -0.5
    trans_b = (((1,), (1,)), ((), ()))

    decays = []
    for h in range(RET_HEADS):
        lg = lg_ref[h]
        decays.append((
            jnp.where(rel >= 0, jnp.exp(lg * rel), 0.0),
            jnp.exp(lg * (pos + 1.0)),
            jnp.exp(lg * (CHUNK - 1.0 - pos)),
            jnp.exp(lg * CHUNK),
        ))

    for ci in range(n_chunks):
        rows = pl.ds(ci * CHUNK, CHUNK)
        cos = cos_ref[rows, :]
        sin = sin_ref[rows, :]
        for h in range(RET_HEADS):
            intra, q_decay, k_decay, chunk_decay = decays[h]
            qk_cols = pl.ds(h * RET_QK_DIM, RET_QK_DIM)
            v_cols = pl.ds(h * RET_V_DIM, RET_V_DIM)
            q = q_ref[rows, qk_cols].astype(F32)
            k = k_ref[rows, qk_cols].astype(F32)
            q = q * cos + pltpu.roll(q, RET_QK_DIM // 2, 1) * sin
            k = (k * cos + pltpu.roll(k, RET_QK_DIM // 2, 1) * sin) * k_scale
            qb = q.astype(BF16)
            v = v_ref[rows, v_cols]
            state = state_ref[h]
            scores = lax.dot_general(qb, k.astype(BF16), trans_b, preferred_element_type=F32) * intra
            out = jnp.dot(scores.astype(BF16), v, preferred_element_type=F32)
            out = out + jnp.dot(qb, state.astype(BF16), preferred_element_type=F32) * q_decay
            kd_t = (k * k_decay).T.astype(BF16)
            state_ref[h] = state * chunk_decay + jnp.dot(kd_t, v, preferred_element_type=F32)
            gate = g_ref[rows, v_cols].astype(F32)
            y = out * _rms_scale(out) * hn_ref[:, v_cols] * (gate * jax.nn.sigmoid(gate))
            o_ref[rows, v_cols] = y.astype(BF16)


def _retention(proj, log_gamma, cos2, sin2, head_norm, batch, seq):
    m = proj.shape[0]
    tr = _pick_tile(seq, 512, CHUNK)
    tiles = seq // tr
    assert RET_V_W == 2 * RET_QK_W
    return pl.pallas_call(
        functools.partial(_retention_kernel, n_chunks=tr // CHUNK),
        out_shape=jax.ShapeDtypeStruct((m, RET_V_W), BF16),
        grid=(batch, tiles),
        in_specs=[
            pl.BlockSpec(memory_space=pltpu.SMEM),
            pl.BlockSpec((tr, RET_QK_W), lambda b, c: (b * tiles + c, 0)),
            pl.BlockSpec((tr, RET_QK_W), lambda b, c: (b * tiles + c, 1)),
            pl.BlockSpec((tr, RET_V_W), lambda b, c: (b * tiles + c, 1)),
            pl.BlockSpec((tr, RET_V_W), lambda b, c: (b * tiles + c, 2)),
            pl.BlockSpec((tr, RET_QK_DIM), lambda b, c: (c, 0)),
            pl.BlockSpec((tr, RET_QK_DIM), lambda b, c: (c, 0)),
            pl.BlockSpec((1, RET_V_W), lambda b, c: (0, 0)),
        ],
        out_specs=pl.BlockSpec((tr, RET_V_W), lambda b, c: (b * tiles + c, 0)),
        scratch_shapes=[pltpu.VMEM((RET_HEADS, RET_QK_DIM, RET_V_DIM), F32)],
        compiler_params=pltpu.CompilerParams(
            dimension_semantics=("parallel", "arbitrary"), vmem_limit_bytes=V7X_VMEM_LIMIT_BYTES
        ),
        name="retention",
    )(log_gamma, proj, proj, proj, proj, cos2, sin2, head_norm.reshape(1, RET_V_W))


def _log_sigmoid(x):
    return jnp.minimum(x, 0.0) - jnp.log1p(jnp.exp(-jnp.abs(x)))


def _mlstm_kernel(ib_ref, fb_ref, q_ref, k_ref, v_ref, og_ref, gt_ref, cwq_ref, cwk_ref, cbq_ref, cbk_ref, hn_ref,
                  o_ref, c_ref, m_ref, qraw_ref, kraw_ref, qs_ref, ks_ref, *, n_chunks):
    c = pl.program_id(1)
    tr = n_chunks * CHUNK
    hist = SUBLANES

    @pl.when(c == 0)
    def _():
        c_ref[...] = jnp.zeros_like(c_ref)
        m_ref[...] = jnp.zeros_like(m_ref)
        qraw_ref[pl.ds(0, hist), :] = jnp.zeros((hist, MLSTM_QK_W), F32)
        kraw_ref[pl.ds(0, hist), :] = jnp.zeros((hist, MLSTM_QK_W), F32)

    def conv_silu(raw_ref, src_ref, w_ref, b_ref, dst_ref, scale):
        raw_ref[pl.ds(hist, tr), :] = src_ref[...].astype(F32)
        for ci in range(n_chunks):
            acc = jnp.broadcast_to(b_ref[...], (CHUNK, MLSTM_QK_W))
            for j in range(CONV_WIDTH):
                taps = raw_ref[pl.ds(ci * CHUNK + hist - (CONV_WIDTH - 1) + j, CHUNK), :]
                acc = acc + taps * w_ref[pl.ds(j, 1), :]
            dst_ref[pl.ds(ci * CHUNK, CHUNK), :] = (acc * jax.nn.sigmoid(acc) * scale).astype(BF16)
        raw_ref[pl.ds(0, hist), :] = raw_ref[pl.ds(tr, hist), :]

    conv_silu(qraw_ref, q_ref, cwq_ref, cbq_ref, qs_ref, 1.0)
    conv_silu(kraw_ref, k_ref, cwk_ref, cbk_ref, ks_ref, MLSTM_QK_DIM ** -0.5)

    row = lax.broadcasted_iota(jnp.int32, (CHUNK, CHUNK), 0)
    col = lax.broadcasted_iota(jnp.int32, (CHUNK, CHUNK), 1)
    causal = col <= row
    lane = lax.broadcasted_iota(jnp.int32, (1, CHUNK), 1)
    ones_col = (lax.broadcasted_iota(jnp.int32, (CHUNK, AUG_W), 1) == 0).astype(BF16)
    trans_b = (((1,), (1,)), ((), ()))

    for ci in range(n_chunks):
        rows = pl.ds(ci * CHUNK, CHUNK)
        gates = gt_ref[:, rows]
        for h in range(MLSTM_HEADS):
            qk_cols = pl.ds(h * MLSTM_QK_DIM, MLSTM_QK_DIM)
            v_cols = pl.ds(h * MLSTM_V_DIM, MLSTM_V_DIM)
            i_row = gates[h:h + 1, :] + ib_ref[h]
            lf_row = _log_sigmoid(gates[MLSTM_HEADS + h:MLSTM_HEADS + h + 1, :] + fb_ref[h])
            b_row = lf_row
            shift = 1
            while shift < CHUNK:
                b_row = b_row + jnp.where(lane >= shift, pltpu.roll(b_row, shift, 1), 0.0)
                shift *= 2
            stacked = jnp.where(row == 0, b_row, jnp.where(row == 1, i_row, 0.0))
            stacked_t = stacked.T
            b_col = stacked_t[:, 0:1]
            i_col = stacked_t[:, 1:2]

            m_prev = m_ref[h, :, 0:1]
            d_log = jnp.where(causal, b_col - b_row + i_row, -jnp.inf)
            inter = b_col + m_prev
            m_t = jnp.maximum(inter, jnp.max(d_log, axis=1, keepdims=True))
            w_inter = jnp.exp(inter - m_t)
            w_intra = jnp.exp(d_log - m_t)

            q = qs_ref[rows, qk_cols]
            k = ks_ref[rows, qk_cols]
            v_aug = jnp.concatenate([v_ref[rows, v_cols], ones_col], axis=1)
            state = c_ref[h]
            s = lax.dot_general(q, k, trans_b, preferred_element_type=F32) * w_intra
            tot = jnp.dot(s.astype(BF16), v_aug, preferred_element_type=F32)
            tot = tot + w_inter * jnp.dot(q, state.astype(BF16), preferred_element_type=F32)
            num = tot[:, :MLSTM_V_DIM]
            nq = tot[:, MLSTM_V_DIM:MLSTM_V_DIM + 1]
            hid = num / jnp.maximum(jnp.abs(nq), jnp.exp(-m_t))

            b_last = b_row[:, CHUNK - 1:CHUNK]
            m_new = m_t[CHUNK - 1:CHUNK, :]
            w_state = jnp.exp(b_last + m_prev - m_new)
            w_key = jnp.exp(b_last - b_col + i_col - m_new)
            kw_t = (k.astype(F32) * w_key).T.astype(BF16)
            c_ref[h] = w_state * state + jnp.dot(kw_t, v_aug, preferred_element_type=F32)
            m_ref[h] = jnp.broadcast_to(m_new, (1, LANES))

            og = og_ref[rows, v_cols].astype(F32)
            y = hid * _rms_scale(hid) * hn_ref[:, v_cols] * jax.nn.sigmoid(og)
            o_ref[rows, v_cols] = y.astype(BF16)


def _mlstm(proj, gates_t, conv_w, conv_b, igate_b, fgate_b, head_norm, batch, seq):
    m = proj.shape[0]
    tr = _pick_tile(seq, 512, CHUNK)
    tiles = seq // tr
    base = 2 * RET_QK_W + 2 * RET_V_W
    assert base % MLSTM_V_W == 0 and MLSTM_V_W == 2 * MLSTM_QK_W
    q_blk = base // MLSTM_QK_W
    v_blk = (base + 2 * MLSTM_QK_W) // MLSTM_V_W
    smem = pl.BlockSpec(memory_space=pltpu.SMEM)
    return pl.pallas_call(
        functools.partial(_mlstm_kernel, n_chunks=tr // CHUNK),
        out_shape=jax.ShapeDtypeStruct((m, MLSTM_V_W), BF16),
        grid=(batch, tiles),
        in_specs=[
            smem,
            smem,
            pl.BlockSpec((tr, MLSTM_QK_W), lambda b, c: (b * tiles + c, q_blk)),
            pl.BlockSpec((tr, MLSTM_QK_W), lambda b, c: (b * tiles + c, q_blk + 1)),
            pl.BlockSpec((tr, MLSTM_V_W), lambda b, c: (b * tiles + c, v_blk)),
            pl.BlockSpec((tr, MLSTM_V_W), lambda b, c: (b * tiles + c, v_blk + 1)),
            pl.BlockSpec((SUBLANES, tr), lambda b, c: (0, b * tiles + c)),
            pl.BlockSpec((CONV_WIDTH, MLSTM_QK_W), lambda b, c: (0, 0)),
            pl.BlockSpec((CONV_WIDTH, MLSTM_QK_W), lambda b, c: (0, 1)),
            pl.BlockSpec((1, MLSTM_QK_W), lambda b, c: (0, 0)),
            pl.BlockSpec((1, MLSTM_QK_W), lambda b, c: (0, 1)),
            pl.BlockSpec((1, MLSTM_V_W), lambda b, c: (0, 0)),
        ],
        out_specs=pl.BlockSpec((tr, MLSTM_V_W), lambda b, c: (b * tiles + c, 0)),
        scratch_shapes=[
            pltpu.VMEM((MLSTM_HEADS, MLSTM_QK_DIM, MLSTM_V_DIM + AUG_W), F32),
            pltpu.VMEM((MLSTM_HEADS, 1, LANES), F32),
            pltpu.VMEM((tr + SUBLANES, MLSTM_QK_W), F32),
            pltpu.VMEM((tr + SUBLANES, MLSTM_QK_W), F32),
            pltpu.VMEM((tr, MLSTM_QK_W), BF16),
            pltpu.VMEM((tr, MLSTM_QK_W), BF16),
        ],
        compiler_params=pltpu.CompilerParams(
            dimension_semantics=("parallel", "arbitrary"), vmem_limit_bytes=V7X_VMEM_LIMIT_BYTES
        ),
        name="mlstm",
    )(igate_b, fgate_b, proj, proj, proj, proj, gates_t, conv_w, conv_w, conv_b.reshape(1, -1),
      conv_b.reshape(1, -1), head_norm.reshape(1, MLSTM_V_W))


def _outproj_kernel(x_ref, r_ref, h_ref, wr_ref, wh_ref, o_ref):
    acc = jnp.dot(r_ref[...], wr_ref[...], preferred_element_type=F32)
    acc = acc + jnp.dot(h_ref[...], wh_ref[...], preferred_element_type=F32)
    o_ref[...] = x_ref[...] + acc


def _out_proj(x2d, ret, mh, w_out):
    m, d = x2d.shape
    tm = _pick_tile(m, 512, SUBLANES)
    tn = _pick_tile(d, 1024, LANES)
    wo = w_out.astype(BF16)
    return pl.pallas_call(
        _outproj_kernel,
        out_shape=jax.ShapeDtypeStruct((m, d), F32),
        grid=(m // tm, d // tn),
        in_specs=[
            pl.BlockSpec((tm, tn), lambda i, n: (i, n)),
            pl.BlockSpec((tm, RET_V_W), lambda i, n: (i, 0)),
            pl.BlockSpec((tm, MLSTM_V_W), lambda i, n: (i, 0)),
            pl.BlockSpec((RET_V_W, tn), lambda i, n: (0, n)),
            pl.BlockSpec((MLSTM_V_W, tn), lambda i, n: (1, n)),
        ],
        out_specs=pl.BlockSpec((tm, tn), lambda i, n: (i, n)),
        compiler_params=pltpu.CompilerParams(
            dimension_semantics=("parallel", "parallel"), vmem_limit_bytes=V7X_VMEM_LIMIT_BYTES
        ),
        name="out_proj",
    )(x2d, ret, mh, wo, wo)


def _rotary_tables(seq):
    half = RET_QK_DIM // 2
    inv_freq = ROPE_BASE ** (-jnp.arange(half, dtype=F32) / half)
    ang = jnp.arange(seq, dtype=F32)[:, None] * inv_freq[None, :]
    cos, sin = jnp.cos(ang), jnp.sin(ang)
    return jnp.concatenate([cos, cos], axis=1), jnp.concatenate([-sin, sin], axis=1)


def kernel(x, ffn1_norm, ffn1_w_gate, ffn1_w_up, ffn1_w_down, mix_norm, w_in, conv_w, conv_b, igate_b, fgate_b,
           ret_head_norm, mlstm_head_norm, w_out, ffn2_norm, ffn2_w_gate, ffn2_w_up, ffn2_w_down, final_norm):
    batch, seq, d = x.shape
    depth = ffn1_norm.shape[0]
    assert seq % CHUNK == 0 and w_in.shape[-1] == PROJ_W + N_GATES and w_out.shape[1] == RET_V_W + MLSTM_V_W
    log_gamma = jnp.log1p(-jnp.exp2(-5.0 - jnp.arange(RET_HEADS, dtype=F32)))
    cos2, sin2 = _rotary_tables(seq)
    h = x.reshape(batch * seq, d)
    for layer in range(depth):
        last = layer == depth - 1
        h, xn = _ffn(h, ffn1_norm[layer], ffn1_w_gate[layer], ffn1_w_up[layer], ffn1_w_down[layer], mix_norm[layer],
                     epilogue="emit")
        proj, gates_t = _in_proj(xn, w_in[layer])
        ret = _retention(proj, log_gamma, cos2, sin2, ret_head_norm[layer], batch, seq)
        mh = _mlstm(proj, gates_t, conv_w[layer], conv_b[layer], igate_b[layer], fgate_b[layer],
                    mlstm_head_norm[layer], batch, seq)
        h = _out_proj(h, ret, mh, w_out[layer])
        h = _ffn(h, ffn2_norm[layer], ffn2_w_gate[layer], ffn2_w_up[layer], ffn2_w_down[layer], final_norm,
                 epilogue="final" if last else "none")
    return h.reshape(batch, seq, d)
```

```python
import functools

import jax
import jax.numpy as jnp
from jax import lax
from jax.experimental import pallas as pl
from jax.experimental.pallas import tpu as pltpu

F32 = jnp.float32
BF16 = jnp.bfloat16

RET_HEADS = 8
RET_QK_DIM = 128
RET_V_DIM = 256
MLSTM_HEADS = 4
MLSTM_QK_DIM = 256
MLSTM_V_DIM = 512
CONV_WIDTH = 4
CHUNK = 128
ROPE_BASE = 10000.0
NORM_EPS = 1e-6
FFN_RES_WEIGHT = 0.5

RET_QK_W = RET_HEADS * RET_QK_DIM
RET_V_W = RET_HEADS * RET_V_DIM
MLSTM_QK_W = MLSTM_HEADS * MLSTM_QK_DIM
MLSTM_V_W = MLSTM_HEADS * MLSTM_V_DIM
PROJ_W = 2 * RET_QK_W + 2 * RET_V_W + 2 * MLSTM_QK_W + 2 * MLSTM_V_W
N_GATES = 2 * MLSTM_HEADS

LANES = 128
SUBLANES = 8
V7X_VMEM_LIMIT_BYTES = 58 * 1024 * 1024

AUG_W = LANES


def _pick_tile(total, preferred, quantum):
    if total <= preferred:
        return total
    t = (preferred // quantum) * quantum
    while t >= quantum:
        if total % t == 0:
            return t
        t -= quantum
    return total


def _rms_scale(x):
    return lax.rsqrt(jnp.mean(x * x, axis=-1, keepdims=True) + NORM_EPS)


FFN_NORM_ROWS = 128


def _ffn_kernel(x_hbm, g_ref, wg_ref, wu_ref, wd_ref, eg_ref, *rest, tm, tf, epilogue):
    if epilogue == "emit":
        o_hbm, n_hbm, acc_ref, xn_ref, sem = rest
    else:
        o_hbm, acc_ref, xn_ref, sem = rest
    i = pl.program_id(0)
    f = pl.program_id(1)
    n_tiles = pl.num_programs(0)
    last_f = pl.num_programs(1) - 1

    n_slabs = tm // FFN_NORM_ROWS

    def slab(r):
        return pl.ds(pl.multiple_of(r * FFN_NORM_ROWS, FFN_NORM_ROWS), FFN_NORM_ROWS)

    def hbm_rows(tile, r):
        return pl.ds(pl.multiple_of(tile * tm + r * FFN_NORM_ROWS, FFN_NORM_ROWS), FFN_NORM_ROWS)

    def x_copy(tile, r):
        return pltpu.make_async_copy(x_hbm.at[hbm_rows(tile, r)], acc_ref.at[slab(r)], sem.at[0, r])

    def y_copy(tile, r):
        return pltpu.make_async_copy(acc_ref.at[slab(r)], o_hbm.at[hbm_rows(tile, r)], sem.at[1, r])

    def n_copy(tile, r):
        return pltpu.make_async_copy(xn_ref.at[slab(r)], n_hbm.at[hbm_rows(tile, r)], sem.at[2, r])

    @pl.when(f == 0)
    def _():
        def load_rows(r, carry):
            @pl.when(i > 0)
            def _():
                y_copy(i - 1, r).wait()

            x_copy(i, r).start()
            return carry

        lax.fori_loop(0, n_slabs, load_rows, 0)

        def norm_rows(r, carry):
            x_copy(i, r).wait()
            if epilogue == "emit":
                @pl.when(i > 0)
                def _():
                    n_copy(i - 1, r).wait()

            x = acc_ref[slab(r), :]
            xn_ref[slab(r), :] = (x * _rms_scale(x) * g_ref[...]).astype(BF16)
            return carry

        lax.fori_loop(0, n_slabs, norm_rows, 0)

    wgu = jnp.concatenate([wg_ref[...].astype(BF16), wu_ref[...].astype(BF16)], axis=1)
    gu = jnp.dot(xn_ref[...], wgu, preferred_element_type=F32)
    g = gu[:, :tf]
    u = gu[:, tf:]
    h = (g * jax.nn.sigmoid(g) * u * FFN_RES_WEIGHT).astype(BF16)
    acc_ref[...] += jnp.dot(h, wd_ref[...].astype(BF16), preferred_element_type=F32)

    @pl.when(f == last_f)
    def _():
        def finish_rows(r, carry):
            if epilogue == "final":
                y = acc_ref[slab(r), :]
                acc_ref[slab(r), :] = y * _rms_scale(y) * eg_ref[...]
            elif epilogue == "emit":
                y = acc_ref[slab(r), :]
                xn_ref[slab(r), :] = (y * _rms_scale(y) * eg_ref[...]).astype(BF16)
                n_copy(i, r).start()
            y_copy(i, r).start()
            return carry

        lax.fori_loop(0, n_slabs, finish_rows, 0)

        @pl.when(i == n_tiles - 1)
        def _():
            def drain_rows(r, carry):
                y_copy(i, r).wait()
                if epilogue == "emit":
                    n_copy(i, r).wait()
                return carry

            lax.fori_loop(0, n_slabs, drain_rows, 0)


def _ffn(x2d, norm_g, w_gate, w_up, w_down, epilogue_g, *, epilogue):
    m, d = x2d.shape
    dff = w_gate.shape[1]
    tm = _pick_tile(m, 1024, FFN_NORM_ROWS)
    tf = _pick_tile(dff, 256, LANES)
    assert tm % FFN_NORM_ROWS == 0 and epilogue in ("none", "final", "emit")
    hbm = pl.BlockSpec(memory_space=pl.ANY)
    y_shape = jax.ShapeDtypeStruct((m, d), F32)
    emit = epilogue == "emit"
    return pl.pallas_call(
        functools.partial(_ffn_kernel, tm=tm, tf=tf, epilogue=epilogue),
        out_shape=(y_shape, jax.ShapeDtypeStruct((m, d), BF16)) if emit else y_shape,
        grid=(m // tm, dff // tf),
        in_specs=[
            hbm,
            pl.BlockSpec((1, d), lambda i, f: (0, 0)),
            pl.BlockSpec((d, tf), lambda i, f: (0, f)),
            pl.BlockSpec((d, tf), lambda i, f: (0, f)),
            pl.BlockSpec((tf, d), lambda i, f: (f, 0)),
            pl.BlockSpec((1, d), lambda i, f: (0, 0)),
        ],
        out_specs=(hbm, hbm) if emit else hbm,
        scratch_shapes=[
            pltpu.VMEM((tm, d), F32),
            pltpu.VMEM((tm, d), BF16),
            pltpu.SemaphoreType.DMA((3, tm // FFN_NORM_ROWS)),
        ],
        compiler_params=pltpu.CompilerParams(
            dimension_semantics=("arbitrary", "arbitrary"), vmem_limit_bytes=V7X_VMEM_LIMIT_BYTES
        ),
        name="ffn_" + epilogue,
    )(x2d, norm_g.reshape(1, d), w_gate, w_up, w_down, epilogue_g.reshape(1, d))


def _inproj_kernel(xn_ref, wt_ref, wgt_ref, p_ref, gt_ref):
    n = pl.program_id(1)
    trans_b = (((1,), (1,)), ((), ()))

    @pl.when(n == 0)
    def _():
        wg = jnp.concatenate([wgt_ref[...], jnp.zeros_like(wgt_ref)], axis=0).astype(BF16)
        gt = lax.dot_general(wg, xn_ref[...], trans_b, preferred_element_type=F32)
        gt_ref[...] = gt[:SUBLANES, :]

    w = wt_ref[...].astype(BF16)
    p_ref[...] = lax.dot_general(xn_ref[...], w, trans_b, preferred_element_type=F32).astype(BF16)


def _in_proj(xn, w_in):
    m, d = xn.shape
    assert N_GATES == SUBLANES and PROJ_W % SUBLANES == 0
    tm = _pick_tile(m, 1024, LANES)
    tn = _pick_tile(PROJ_W, 768, 2 * LANES)
    w_t = w_in.T
    return pl.pallas_call(
        _inproj_kernel,
        out_shape=(
            jax.ShapeDtypeStruct((m, PROJ_W), BF16),
            jax.ShapeDtypeStruct((SUBLANES, m), F32),
        ),
        grid=(m // tm, PROJ_W // tn),
        in_specs=[
            pl.BlockSpec((tm, d), lambda i, n: (i, 0)),
            pl.BlockSpec((tn, d), lambda i, n: (n, 0)),
            pl.BlockSpec((N_GATES, d), lambda i, n: (PROJ_W // N_GATES, 0)),
        ],
        out_specs=(
            pl.BlockSpec((tm, tn), lambda i, n: (i, n)),
            pl.BlockSpec((SUBLANES, tm), lambda i, n: (0, i)),
        ),
        compiler_params=pltpu.CompilerParams(
            dimension_semantics=("parallel", "arbitrary"), vmem_limit_bytes=V7X_VMEM_LIMIT_BYTES
        ),
        name="in_proj",
    )(xn, w_t, w_t)


def _retention_kernel(lg_ref, q_ref, k_ref, v_ref, g_ref, cos_ref, sin_ref, hn_ref, o_ref, state_ref, *, n_chunks):
    c = pl.program_id(1)

    @pl.when(c == 0)
    def _():
        state_ref[...] = jnp.zeros_like(state_ref)

    row = lax.broadcasted_iota(jnp.int32, (CHUNK, CHUNK), 0)
    col = lax.broadcasted_iota(jnp.int32, (CHUNK, CHUNK), 1)
    rel = (row - col).astype(F32)
    pos = lax.broadcasted_iota(jnp.int32, (CHUNK, 1), 0).astype(F32)
    k_scale = RET_QK_DIM ** -0.5
    trans_b = (((1,), (1,)), ((), ()))

    decays = []
    for h in range(RET_HEADS):
        lg = lg_ref[h]
        decays.append((
            jnp.where(rel >= 0, jnp.exp(lg * rel), 0.0),
            jnp.exp(lg * (pos + 1.0)),
            jnp.exp(lg * (CHUNK - 1.0 - pos)),
            jnp.exp(lg * CHUNK),
        ))

    for ci in range(n_chunks):
        rows = pl.ds(ci * CHUNK, CHUNK)
        cos = cos_ref[rows, :]
        sin = sin_ref[rows, :]
        for h in range(RET_HEADS):
            intra, q_decay, k_decay, chunk_decay = decays[h]
            qk_cols = pl.ds(h * RET_QK_DIM, RET_QK_DIM)
            v_cols = pl.ds(h * RET_V_DIM, RET_V_DIM)
            q = q_ref[rows, qk_cols].astype(F32)
            k = k_ref[rows, qk_cols].astype(F32)
            q = q * cos + pltpu.roll(q, RET_QK_DIM // 2, 1) * sin
            k = (k * cos + pltpu.roll(k, RET_QK_DIM // 2, 1) * sin) * k_scale
            qb = q.astype(BF16)
            v = v_ref[rows, v_cols]
            state = state_ref[h]
            scores = lax.dot_general(qb, k.astype(BF16), trans_b, preferred_element_type=F32) * intra
            out = jnp.dot(scores.astype(BF16), v, preferred_element_type=F32)
            out = out + jnp.dot(qb, state.astype(BF16), preferred_element_type=F32) * q_decay
            kd_t = (k * k_decay).T.astype(BF16)
            state_ref[h] = state * chunk_decay + jnp.dot(kd_t, v, preferred_element_type=F32)
            gate = g_ref[rows, v_cols].astype(F32)
            y = out * _rms_scale(out) * hn_ref[:, v_cols] * (gate * jax.nn.sigmoid(gate))
            o_ref[rows, v_cols] = y.astype(BF16)


def _retention(proj, log_gamma, cos2, sin2, head_norm, batch, seq):
    m = proj.shape[0]
    tr = _pick_tile(seq, 512, CHUNK)
    tiles = seq // tr
    assert RET_V_W == 2 * RET_QK_W
    return pl.pallas_call(
        functools.partial(_retention_kernel, n_chunks=tr // CHUNK),
        out_shape=jax.ShapeDtypeStruct((m, RET_V_W), BF16),
        grid=(batch, tiles),
        in_specs=[
            pl.BlockSpec(memory_space=pltpu.SMEM),
            pl.BlockSpec((tr, RET_QK_W), lambda b, c: (b * tiles + c, 0)),
            pl.BlockSpec((tr, RET_QK_W), lambda b, c: (b * tiles + c, 1)),
            pl.BlockSpec((tr, RET_V_W), lambda b, c: (b * tiles + c, 1)),
            pl.BlockSpec((tr, RET_V_W), lambda b, c: (b * tiles + c, 2)),
            pl.BlockSpec((tr, RET_QK_DIM), lambda b, c: (c, 0)),
            pl.BlockSpec((tr, RET_QK_DIM), lambda b, c: (c, 0)),
            pl.BlockSpec((1, RET_V_W), lambda b, c: (0, 0)),
        ],
        out_specs=pl.BlockSpec((tr, RET_V_W), lambda b, c: (b * tiles + c, 0)),
        scratch_shapes=[pltpu.VMEM((RET_HEADS, RET_QK_DIM, RET_V_DIM), F32)],
        compiler_params=pltpu.CompilerParams(
            dimension_semantics=("parallel", "arbitrary"), vmem_limit_bytes=V7X_VMEM_LIMIT_BYTES
        ),
        name="retention",
    )(log_gamma, proj, proj, proj, proj, cos2, sin2, head_norm.reshape(1, RET_V_W))


def _log_sigmoid(x):
    return jnp.minimum(x, 0.0) - jnp.log1p(jnp.exp(-jnp.abs(x)))


def _mlstm_kernel(ib_ref, fb_ref, q_ref, k_ref, v_ref, og_ref, gt_ref, cwq_ref, cwk_ref, cbq_ref, cbk_ref, hn_ref,
                  o_ref, c_ref, m_ref, qraw_ref, kraw_ref, qs_ref, ks_ref, *, n_chunks):
    c = pl.program_id(1)
    tr = n_chunks * CHUNK
    hist = SUBLANES

    @pl.when(c == 0)
    def _():
        c_ref[...] = jnp.zeros_like(c_ref)
        m_ref[...] = jnp.zeros_like(m_ref)
        qraw_ref[pl.ds(0, hist), :] = jnp.zeros((hist, MLSTM_QK_W), F32)
        kraw_ref[pl.ds(0, hist), :] = jnp.zeros((hist, MLSTM_QK_W), F32)

    def conv_silu(raw_ref, src_ref, w_ref, b_ref, dst_ref, scale):
        raw_ref[pl.ds(hist, tr), :] = src_ref[...].astype(F32)
        for ci in range(n_chunks):
            acc = jnp.broadcast_to(b_ref[...], (CHUNK, MLSTM_QK_W))
            for j in range(CONV_WIDTH):
                taps = raw_ref[pl.ds(ci * CHUNK + hist - (CONV_WIDTH - 1) + j, CHUNK), :]
                acc = acc + taps * w_ref[pl.ds(j, 1), :]
            dst_ref[pl.ds(ci * CHUNK, CHUNK), :] = (acc * jax.nn.sigmoid(acc) * scale).astype(BF16)
        raw_ref[pl.ds(0, hist), :] = raw_ref[pl.ds(tr, hist), :]

    conv_silu(qraw_ref, q_ref, cwq_ref, cbq_ref, qs_ref, 1.0)
    conv_silu(kraw_ref, k_ref, cwk_ref, cbk_ref, ks_ref, MLSTM_QK_DIM ** -0.5)

    row = lax.broadcasted_iota(jnp.int32, (CHUNK, CHUNK), 0)
    col = lax.broadcasted_iota(jnp.int32, (CHUNK, CHUNK), 1)
    causal = col <= row
    lane = lax.broadcasted_iota(jnp.int32, (1, CHUNK), 1)
    ones_col = (lax.broadcasted_iota(jnp.int32, (CHUNK, AUG_W), 1) == 0).astype(BF16)
    trans_b = (((1,), (1,)), ((), ()))

    for ci in range(n_chunks):
        rows = pl.ds(ci * CHUNK, CHUNK)
        gates = gt_ref[:, rows]
        for h in range(MLSTM_HEADS):
            qk_cols = pl.ds(h * MLSTM_QK_DIM, MLSTM_QK_DIM)
            v_cols = pl.ds(h * MLSTM_V_DIM, MLSTM_V_DIM)
            i_row = gates[h:h + 1, :] + ib_ref[h]
            lf_row = _log_sigmoid(gates[MLSTM_HEADS + h:MLSTM_HEADS + h + 1, :] + fb_ref[h])
            b_row = lf_row
            shift = 1
            while shift < CHUNK:
                b_row = b_row + jnp.where(lane >= shift, pltpu.roll(b_row, shift, 1), 0.0)
                shift *= 2
            stacked = jnp.where(row == 0, b_row, jnp.where(row == 1, i_row, 0.0))
            stacked_t = stacked.T
            b_col = stacked_t[:, 0:1]
            i_col = stacked_t[:, 1:2]

            m_prev = m_ref[h, :, 0:1]
            d_log = jnp.where(causal, b_col - b_row + i_row, -jnp.inf)
            inter = b_col + m_prev
            m_t = jnp.maximum(inter, jnp.max(d_log, axis=1, keepdims=True))
            w_inter = jnp.exp(inter - m_t)
            w_intra = jnp.exp(d_log - m_t)

            q = qs_ref[rows, qk_cols]
            k = ks_ref[rows, qk_cols]
            v_aug = jnp.concatenate([v_ref[rows, v_cols], ones_col], axis=1)
            state = c_ref[h]
            s = lax.dot_general(q, k, trans_b, preferred_element_type=F32) * w_intra
            tot = jnp.dot(s.astype(BF16), v_aug, preferred_element_type=F32)
            tot = tot + w_inter * jnp.dot(q, state.astype(BF16), preferred_element_type=F32)
            num = tot[:, :MLSTM_V_DIM]
            nq = tot[:, MLSTM_V_DIM:MLSTM_V_DIM + 1]
            hid = num / jnp.maximum(jnp.abs(nq), jnp.exp(-m_t))

            b_last = b_row[:, CHUNK - 1:CHUNK]
            m_new = m_t[CHUNK - 1:CHUNK, :]
            w_state = jnp.exp(b_last + m_prev - m_new)
            w_key = jnp.exp(b_last - b_col + i_col - m_new)
            kw_t = (k.astype(F32) * w_key).T.astype(BF16)
            c_ref[h] = w_state * state + jnp.dot(kw_t, v_aug, preferred_element_type=F32)
            m_ref[h] = jnp.broadcast_to(m_new, (1, LANES))

            og = og_ref[rows, v_cols].astype(F32)
            y = hid * _rms_scale(hid) * hn_ref[:, v_cols] * jax.nn.sigmoid(og)
            o_ref[rows, v_cols] = y.astype(BF16)


def _mlstm(proj, gates_t, conv_w, conv_b, igate_b, fgate_b, head_norm, batch, seq):
    m = proj.shape[0]
    tr = _pick_tile(seq, 512, CHUNK)
    tiles = seq // tr
    base = 2 * RET_QK_W + 2 * RET_V_W
    assert base % MLSTM_V_W == 0 and MLSTM_V_W == 2 * MLSTM_QK_W
    q_blk = base // MLSTM_QK_W
    v_blk = (base + 2 * MLSTM_QK_W) // MLSTM_V_W
    smem = pl.BlockSpec(memory_space=pltpu.SMEM)
    return pl.pallas_call(
        functools.partial(_mlstm_kernel, n_chunks=tr // CHUNK),
        out_shape=jax.ShapeDtypeStruct((m, MLSTM_V_W), BF16),
        grid=(batch, tiles),
        in_specs=[
            smem,
            smem,
            pl.BlockSpec((tr, MLSTM_QK_W), lambda b, c: (b * tiles + c, q_blk)),
            pl.BlockSpec((tr, MLSTM_QK_W), lambda b, c: (b * tiles + c, q_blk + 1)),
            pl.BlockSpec((tr, MLSTM_V_W), lambda b, c: (b * tiles + c, v_blk)),
            pl.BlockSpec((tr, MLSTM_V_W), lambda b, c: (b * tiles + c, v_blk + 1)),
            pl.BlockSpec((SUBLANES, tr), lambda b, c: (0, b * tiles + c)),
            pl.BlockSpec((CONV_WIDTH, MLSTM_QK_W), lambda b, c: (0, 0)),
            pl.BlockSpec((CONV_WIDTH, MLSTM_QK_W), lambda b, c: (0, 1)),
            pl.BlockSpec((1, MLSTM_QK_W), lambda b, c: (0, 0)),
            pl.BlockSpec((1, MLSTM_QK_W), lambda b, c: (0, 1)),
            pl.BlockSpec((1, MLSTM_V_W), lambda b, c: (0, 0)),
        ],
        out_specs=pl.BlockSpec((tr, MLSTM_V_W), lambda b, c: (b * tiles + c, 0)),
        scratch_shapes=[
            pltpu.VMEM((MLSTM_HEADS, MLSTM_QK_DIM, MLSTM_V_DIM + AUG_W), F32),
            pltpu.VMEM((MLSTM_HEADS, 1, LANES), F32),
            pltpu.VMEM((tr + SUBLANES, MLSTM_QK_W), F32),
            pltpu.VMEM((tr + SUBLANES, MLSTM_QK_W), F32),
            pltpu.VMEM((tr, MLSTM_QK_W), BF16),
            pltpu.VMEM((tr, MLSTM_QK_W), BF16),
        ],
        compiler_params=pltpu.CompilerParams(
            dimension_semantics=("parallel", "arbitrary"), vmem_limit_bytes=V7X_VMEM_LIMIT_BYTES
        ),
        name="mlstm",
    )(igate_b, fgate_b, proj, proj, proj, proj, gates_t, conv_w, conv_w, conv_b.reshape(1, -1),
      conv_b.reshape(1, -1), head_norm.reshape(1, MLSTM_V_W))


def _outproj_kernel(x_ref, r_ref, h_ref, wr_ref, wh_ref, o_ref):
    acc = jnp.dot(r_ref[...], wr_ref[...], preferred_element_type=F32)
    acc = acc + jnp.dot(h_ref[...], wh_ref[...], preferred_element_type=F32)
    o_ref[...] = x_ref[...] + acc


def _out_proj(x2d, ret, mh, w_out):
    m, d = x2d.shape
    tm = _pick_tile(m, 1024, SUBLANES)
    tn = _pick_tile(d, 1024, LANES)
    wo = w_out.astype(BF16)
    return pl.pallas_call(
        _outproj_kernel,
        out_shape=jax.ShapeDtypeStruct((m, d), F32),
        grid=(m // tm, d // tn),
        in_specs=[
            pl.BlockSpec((tm, tn), lambda i, n: (i, n)),
            pl.BlockSpec((tm, RET_V_W), lambda i, n: (i, 0)),
            pl.BlockSpec((tm, MLSTM_V_W), lambda i, n: (i, 0)),
            pl.BlockSpec((RET_V_W, tn), lambda i, n: (0, n)),
            pl.BlockSpec((MLSTM_V_W, tn), lambda i, n: (1, n)),
        ],
        out_specs=pl.BlockSpec((tm, tn), lambda i, n: (i, n)),
        compiler_params=pltpu.CompilerParams(
            dimension_semantics=("parallel", "parallel"), vmem_limit_bytes=V7X_VMEM_LIMIT_BYTES
        ),
        name="out_proj",
    )(x2d, ret, mh, wo, wo)


def _rotary_tables(seq):
    half = RET_QK_DIM // 2
    inv_freq = ROPE_BASE ** (-jnp.arange(half, dtype=F32) / half)
    ang = jnp.arange(seq, dtype=F32)[:, None] * inv_freq[None, :]
    cos, sin = jnp.cos(ang), jnp.sin(ang)
    return jnp.concatenate([cos, cos], axis=1), jnp.concatenate([-sin, sin], axis=1)


def kernel(x, ffn1_norm, ffn1_w_gate, ffn1_w_up, ffn1_w_down, mix_norm, w_in, conv_w, conv_b, igate_b, fgate_b,
           ret_head_norm, mlstm_head_norm, w_out, ffn2_norm, ffn2_w_gate, ffn2_w_up, ffn2_w_down, final_norm):
    batch, seq, d = x.shape
    depth = ffn1_norm.shape[0]
    assert seq % CHUNK == 0 and w_in.shape[-1] == PROJ_W + N_GATES and w_out.shape[1] == RET_V_W + MLSTM_V_W
    log_gamma = jnp.log1p(-jnp.exp2(-5.0 - jnp.arange(RET_HEADS, dtype=F32)))
    cos2, sin2 = _rotary_tables(seq)
    h = x.reshape(batch * seq, d)
    for layer in range(depth):
        last = layer == depth - 1
        h, xn = _ffn(h, ffn1_norm[layer], ffn1_w_gate[layer], ffn1_w_up[layer], ffn1_w_down[layer], mix_norm[layer],
                     epilogue="emit")
        proj, gates_t = _in_proj(xn, w_in[layer])
        ret = _retention(proj, log_gamma, cos2, sin2, ret_head_norm[layer], batch, seq)
        mh = _mlstm(proj, gates_t, conv_w[layer], conv_b[layer], igate_b[layer], fgate_b[layer],
                    mlstm_head_norm[layer], batch, seq)
        h = _out_proj(h, ret, mh, w_out[layer])
        h = _ffn(h, ffn2_norm[layer], ffn2_w_gate[layer], ffn2_w_up[layer], ffn2_w_down[layer], final_norm,
                 epilogue="final" if last else "none")
    return h.reshape(batch, seq, d)
```

```python
import functools

import jax
import jax.numpy as jnp
from jax import lax
from jax.experimental import pallas as pl
from jax.experimental.pallas import tpu as pltpu

F32 = jnp.float32
BF16 = jnp.bfloat16

RET_HEADS = 8
RET_QK_DIM = 128
RET_V_DIM = 256
MLSTM_HEADS = 4
MLSTM_QK_DIM = 256
MLSTM_V_DIM = 512
CONV_WIDTH = 4
CHUNK = 128
ROPE_BASE = 10000.0
NORM_EPS = 1e-6
FFN_RES_WEIGHT = 0.5

RET_QK_W = RET_HEADS * RET_QK_DIM
RET_V_W = RET_HEADS * RET_V_DIM
MLSTM_QK_W = MLSTM_HEADS * MLSTM_QK_DIM
MLSTM_V_W = MLSTM_HEADS * MLSTM_V_DIM
PROJ_W = 2 * RET_QK_W + 2 * RET_V_W + 2 * MLSTM_QK_W + 2 * MLSTM_V_W
N_GATES = 2 * MLSTM_HEADS

LANES = 128
SUBLANES = 8
V7X_VMEM_LIMIT_BYTES = 58 * 1024 * 1024

AUG_W = LANES


def _pick_tile(total, preferred, quantum):
    if total <= preferred:
        return total
    t = (preferred // quantum) * quantum
    while t >= quantum:
        if total % t == 0:
            return t
        t -= quantum
    return total


def _rms_scale(x):
    return lax.rsqrt(jnp.mean(x * x, axis=-1, keepdims=True) + NORM_EPS)


FFN_NORM_ROWS = 128


def _ffn_kernel(x_hbm, g_ref, wg_ref, wu_ref, wd_ref, eg_ref, *rest, tm, tf, epilogue):
    if epilogue == "emit":
        o_hbm, n_hbm, acc_ref, xn_ref, sem = rest
    else:
        o_hbm, acc_ref, xn_ref, sem = rest
    i = pl.program_id(0)
    f = pl.program_id(1)
    n_tiles = pl.num_programs(0)
    last_f = pl.num_programs(1) - 1

    n_slabs = tm // FFN_NORM_ROWS

    def slab(r):
        return pl.ds(pl.multiple_of(r * FFN_NORM_ROWS, FFN_NORM_ROWS), FFN_NORM_ROWS)

    def hbm_rows(tile, r):
        return pl.ds(pl.multiple_of(tile * tm + r * FFN_NORM_ROWS, FFN_NORM_ROWS), FFN_NORM_ROWS)

    def x_copy(tile, r):
        return pltpu.make_async_copy(x_hbm.at[hbm_rows(tile, r)], acc_ref.at[slab(r)], sem.at[0, r])

    def y_copy(tile, r):
        return pltpu.make_async_copy(acc_ref.at[slab(r)], o_hbm.at[hbm_rows(tile, r)], sem.at[1, r])

    def n_copy(tile, r):
        return pltpu.make_async_copy(xn_ref.at[slab(r)], n_hbm.at[hbm_rows(tile, r)], sem.at[2, r])

    @pl.when(f == 0)
    def _():
        def load_rows(r, carry):
            @pl.when(i > 0)
            def _():
                y_copy(i - 1, r).wait()

            x_copy(i, r).start()
            return carry

        lax.fori_loop(0, n_slabs, load_rows, 0)

        def norm_rows(r, carry):
            x_copy(i, r).wait()
            if epilogue == "emit":
                @pl.when(i > 0)
                def _():
                    n_copy(i - 1, r).wait()

            x = acc_ref[slab(r), :]
            xn_ref[slab(r), :] = (x * _rms_scale(x) * g_ref[...]).astype(BF16)
            return carry

        lax.fori_loop(0, n_slabs, norm_rows, 0)

    wgu = jnp.concatenate([wg_ref[...].astype(BF16), wu_ref[...].astype(BF16)], axis=1)
    gu = jnp.dot(xn_ref[...], wgu, preferred_element_type=F32)
    g = gu[:, :tf]
    u = gu[:, tf:]
    h = (g * jax.nn.sigmoid(g) * u * FFN_RES_WEIGHT).astype(BF16)
    acc_ref[...] += jnp.dot(h, wd_ref[...].astype(BF16), preferred_element_type=F32)

    @pl.when(f == last_f)
    def _():
        def finish_rows(r, carry):
            if epilogue == "final":
                y = acc_ref[slab(r), :]
                acc_ref[slab(r), :] = y * _rms_scale(y) * eg_ref[...]
            elif epilogue == "emit":
                y = acc_ref[slab(r), :]
                xn_ref[slab(r), :] = (y * _rms_scale(y) * eg_ref[...]).astype(BF16)
                n_copy(i, r).start()
            y_copy(i, r).start()
            return carry

        lax.fori_loop(0, n_slabs, finish_rows, 0)

        @pl.when(i == n_tiles - 1)
        def _():
            def drain_rows(r, carry):
                y_copy(i, r).wait()
                if epilogue == "emit":
                    n_copy(i, r).wait()
                return carry

            lax.fori_loop(0, n_slabs, drain_rows, 0)


def _ffn(x2d, norm_g, w_gate, w_up, w_down, epilogue_g, *, epilogue):
    m, d = x2d.shape
    dff = w_gate.shape[1]
    tm = _pick_tile(m, 1024, FFN_NORM_ROWS)
    tf = _pick_tile(dff, 256, LANES)
    assert tm % FFN_NORM_ROWS == 0 and epilogue in ("none", "final", "emit")
    hbm = pl.BlockSpec(memory_space=pl.ANY)
    y_shape = jax.ShapeDtypeStruct((m, d), F32)
    emit = epilogue == "emit"
    return pl.pallas_call(
        functools.partial(_ffn_kernel, tm=tm, tf=tf, epilogue=epilogue),
        out_shape=(y_shape, jax.ShapeDtypeStruct((m, d), BF16)) if emit else y_shape,
        grid=(m // tm, dff // tf),
        in_specs=[
            hbm,
            pl.BlockSpec((1, d), lambda i, f: (0, 0)),
            pl.BlockSpec((d, tf), lambda i, f: (0, f)),
            pl.BlockSpec((d, tf), lambda i, f: (0, f)),
            pl.BlockSpec((tf, d), lambda i, f: (f, 0)),
            pl.BlockSpec((1, d), lambda i, f: (0, 0)),
        ],
        out_specs=(hbm, hbm) if emit else hbm,
        scratch_shapes=[
            pltpu.VMEM((tm, d), F32),
            pltpu.VMEM((tm, d), BF16),
            pltpu.SemaphoreType.DMA((3, tm // FFN_NORM_ROWS)),
        ],
        compiler_params=pltpu.CompilerParams(
            dimension_semantics=("arbitrary", "arbitrary"), vmem_limit_bytes=V7X_VMEM_LIMIT_BYTES
        ),
        name="ffn_" + epilogue,
    )(x2d, norm_g.reshape(1, d), w_gate, w_up, w_down, epilogue_g.reshape(1, d))


def _inproj_kernel(xn_ref, wt_ref, wgt_ref, p_ref, gt_ref):
    n = pl.program_id(1)
    trans_b = (((1,), (1,)), ((), ()))

    @pl.when(n == 0)
    def _():
        wg = jnp.concatenate([wgt_ref[...], jnp.zeros_like(wgt_ref)], axis=0).astype(BF16)
        gt = lax.dot_general(wg, xn_ref[...], trans_b, preferred_element_type=F32)
        gt_ref[...] = gt[:SUBLANES, :]

    w = wt_ref[...].astype(BF16)
    p_ref[...] = lax.dot_general(xn_ref[...], w, trans_b, preferred_element_type=F32).astype(BF16)


def _in_proj(xn, w_in):
    m, d = xn.shape
    assert N_GATES == SUBLANES and PROJ_W % SUBLANES == 0
    tm = _pick_tile(m, 1024, LANES)
    tn = _pick_tile(PROJ_W, 768, 2 * LANES)
    w_t = w_in.T
    return pl.pallas_call(
        _inproj_kernel,
        out_shape=(
            jax.ShapeDtypeStruct((m, PROJ_W), BF16),
            jax.ShapeDtypeStruct((SUBLANES, m), F32),
        ),
        grid=(m // tm, PROJ_W // tn),
        in_specs=[
            pl.BlockSpec((tm, d), lambda i, n: (i, 0)),
            pl.BlockSpec((tn, d), lambda i, n: (n, 0)),
            pl.BlockSpec((N_GATES, d), lambda i, n: (PROJ_W // N_GATES, 0)),
        ],
        out_specs=(
            pl.BlockSpec((tm, tn), lambda i, n: (i, n)),
            pl.BlockSpec((SUBLANES, tm), lambda i, n: (0, i)),
        ),
        compiler_params=pltpu.CompilerParams(
            dimension_semantics=("parallel", "arbitrary"), vmem_limit_bytes=V7X_VMEM_LIMIT_BYTES
        ),
        name="in_proj",
    )(xn, w_t, w_t)


def _retention_body(lg_ref, q_ref, k_ref, v_ref, g_ref, cos_ref, sin_ref, hn_ref, o_ref, state_ref, *, n_chunks):
    row = lax.broadcasted_iota(jnp.int32, (CHUNK, CHUNK), 0)
    col = lax.broadcasted_iota(jnp.int32, (CHUNK, CHUNK), 1)
    rel = (row - col).astype(F32)
    pos = lax.broadcasted_iota(jnp.int32, (CHUNK, 1), 0).astype(F32)
    k_scale = RET_QK_DIM ** -0.5
    trans_b = (((1,), (1,)), ((), ()))

    decays = []
    for h in range(RET_HEADS):
        lg = lg_ref[h]
        decays.append((
            jnp.where(rel >= 0, jnp.exp(lg * rel), 0.0),
            jnp.exp(lg * (pos + 1.0)),
            jnp.exp(lg * (CHUNK - 1.0 - pos)),
            jnp.exp(lg * CHUNK),
        ))

    for ci in range(n_chunks):
        rows = pl.ds(ci * CHUNK, CHUNK)
        cos = cos_ref[rows, :]
        sin = sin_ref[rows, :]
        for h in range(RET_HEADS):
            intra, q_decay, k_decay, chunk_decay = decays[h]
            qk_cols = pl.ds(h * RET_QK_DIM, RET_QK_DIM)
            v_cols = pl.ds(h * RET_V_DIM, RET_V_DIM)
            q = q_ref[rows, qk_cols].astype(F32)
            k = k_ref[rows, qk_cols].astype(F32)
            q = q * cos + pltpu.roll(q, RET_QK_DIM // 2, 1) * sin
            k = (k * cos + pltpu.roll(k, RET_QK_DIM // 2, 1) * sin) * k_scale
            qb = q.astype(BF16)
            v = v_ref[rows, v_cols]
            state = state_ref[h]
            scores = lax.dot_general(qb, k.astype(BF16), trans_b, preferred_element_type=F32) * intra
            out = jnp.dot(scores.astype(BF16), v, preferred_element_type=F32)
            out = out + jnp.dot(qb, state.astype(BF16), preferred_element_type=F32) * q_decay
            kd_t = (k * k_decay).T.astype(BF16)
            state_ref[h] = state * chunk_decay + jnp.dot(kd_t, v, preferred_element_type=F32)
            gate = g_ref[rows, v_cols].astype(F32)
            y = out * _rms_scale(out) * hn_ref[:, v_cols] * (gate * jax.nn.sigmoid(gate))
            o_ref[rows, v_cols] = y.astype(BF16)


CONV_HIST = SUBLANES


def _log_sigmoid(x):
    return jnp.minimum(x, 0.0) - jnp.log1p(jnp.exp(-jnp.abs(x)))


def _mlstm_body(ib_ref, fb_ref, q_ref, k_ref, v_ref, og_ref, gt_ref, cwq_ref, cwk_ref, cbq_ref, cbk_ref, hn_ref,
                o_ref, c_ref, m_ref, qraw_ref, kraw_ref, qs_ref, ks_ref, *, n_chunks):
    tr = n_chunks * CHUNK
    hist = CONV_HIST

    def conv_silu(raw_ref, src_ref, w_ref, b_ref, dst_ref, scale):
        raw_ref[pl.ds(hist, tr), :] = src_ref[...].astype(F32)
        for ci in range(n_chunks):
            acc = jnp.broadcast_to(b_ref[...], (CHUNK, MLSTM_QK_W))
            for j in range(CONV_WIDTH):
                taps = raw_ref[pl.ds(ci * CHUNK + hist - (CONV_WIDTH - 1) + j, CHUNK), :]
                acc = acc + taps * w_ref[pl.ds(j, 1), :]
            dst_ref[pl.ds(ci * CHUNK, CHUNK), :] = (acc * jax.nn.sigmoid(acc) * scale).astype(BF16)
        raw_ref[pl.ds(0, hist), :] = raw_ref[pl.ds(tr, hist), :]

    conv_silu(qraw_ref, q_ref, cwq_ref, cbq_ref, qs_ref, 1.0)
    conv_silu(kraw_ref, k_ref, cwk_ref, cbk_ref, ks_ref, MLSTM_QK_DIM ** -0.5)

    row = lax.broadcasted_iota(jnp.int32, (CHUNK, CHUNK), 0)
    col = lax.broadcasted_iota(jnp.int32, (CHUNK, CHUNK), 1)
    causal = col <= row
    ones_col = (lax.broadcasted_iota(jnp.int32, (CHUNK, AUG_W), 1) == 0).astype(BF16)
    trans_b = (((1,), (1,)), ((), ()))

    sub = lax.broadcasted_iota(jnp.int32, (SUBLANES, tr), 0)
    pos = lax.broadcasted_iota(jnp.int32, (SUBLANES, tr), 1) % CHUNK
    bias = jnp.zeros((SUBLANES, tr), F32)
    for h in range(MLSTM_HEADS):
        bias = jnp.where(sub == h, ib_ref[h], bias)
        bias = jnp.where(sub == MLSTM_HEADS + h, fb_ref[h], bias)
    pre = gt_ref[...] + bias
    cum = _log_sigmoid(pre)
    shift = 1
    while shift < CHUNK:
        cum = cum + jnp.where(pos >= shift, pltpu.roll(cum, shift, 1), 0.0)
        shift *= 2
    gate_rows = jnp.where(sub < MLSTM_HEADS, pre, cum)
    pad_rows = jnp.zeros((CHUNK - SUBLANES, CHUNK), F32)

    for ci in range(n_chunks):
        rows = pl.ds(ci * CHUNK, CHUNK)
        g_rows = gate_rows[:, ci * CHUNK:(ci + 1) * CHUNK]
        g_cols = jnp.concatenate([g_rows, pad_rows], axis=0).T
        for h in range(MLSTM_HEADS):
            qk_cols = pl.ds(h * MLSTM_QK_DIM, MLSTM_QK_DIM)
            v_cols = pl.ds(h * MLSTM_V_DIM, MLSTM_V_DIM)
            i_row = g_rows[h:h + 1, :]
            b_row = g_rows[MLSTM_HEADS + h:MLSTM_HEADS + h + 1, :]
            i_col = g_cols[:, h:h + 1]
            b_col = g_cols[:, MLSTM_HEADS + h:MLSTM_HEADS + h + 1]

            m_prev = m_ref[h, :, 0:1]
            d_log = jnp.where(causal, b_col - b_row + i_row, -jnp.inf)
            inter = b_col + m_prev
            m_t = jnp.maximum(inter, jnp.max(d_log, axis=1, keepdims=True))
            w_inter = jnp.exp(inter - m_t)
            w_intra = jnp.exp(d_log - m_t)

            q = qs_ref[rows, qk_cols]
            k = ks_ref[rows, qk_cols]
            v_aug = jnp.concatenate([v_ref[rows, v_cols], ones_col], axis=1)
            state = c_ref[h]
            s = lax.dot_general(q, k, trans_b, preferred_element_type=F32) * w_intra
            tot = jnp.dot(s.astype(BF16), v_aug, preferred_element_type=F32)
            tot = tot + w_inter * jnp.dot(q, state.astype(BF16), preferred_element_type=F32)
            num = tot[:, :MLSTM_V_DIM]
            nq = tot[:, MLSTM_V_DIM:MLSTM_V_DIM + 1]
            hid = num / jnp.maximum(jnp.abs(nq), jnp.exp(-m_t))

            b_last = b_row[:, CHUNK - 1:CHUNK]
            m_new = m_t[CHUNK - 1:CHUNK, :]
            w_state = jnp.exp(b_last + m_prev - m_new)
            w_key = jnp.exp(b_last - b_col + i_col - m_new)
            kw_t = (k.astype(F32) * w_key).T.astype(BF16)
            c_ref[h] = w_state * state + jnp.dot(kw_t, v_aug, preferred_element_type=F32)
            m_ref[h] = jnp.broadcast_to(m_new, (1, LANES))

            og = og_ref[rows, v_cols].astype(F32)
            y = hid * _rms_scale(hid) * hn_ref[:, v_cols] * jax.nn.sigmoid(og)
            o_ref[rows, v_cols] = y.astype(BF16)


N_RET_IN = 8
N_MLSTM_IN = 12


def _mixer_kernel(*refs, n_chunks):
    ret_in = refs[:N_RET_IN]
    mlstm_in = refs[N_RET_IN:N_RET_IN + N_MLSTM_IN]
    ret_o, mh_o, state_ref, c_ref, m_ref, qraw_ref, kraw_ref, qs_ref, ks_ref = refs[N_RET_IN + N_MLSTM_IN:]

    @pl.when(pl.program_id(1) == 0)
    def _():
        state_ref[...] = jnp.zeros_like(state_ref)
        c_ref[...] = jnp.zeros_like(c_ref)
        m_ref[...] = jnp.zeros_like(m_ref)
        qraw_ref[pl.ds(0, CONV_HIST), :] = jnp.zeros((CONV_HIST, MLSTM_QK_W), F32)
        kraw_ref[pl.ds(0, CONV_HIST), :] = jnp.zeros((CONV_HIST, MLSTM_QK_W), F32)

    _retention_body(*ret_in, ret_o, state_ref, n_chunks=n_chunks)
    _mlstm_body(*mlstm_in, mh_o, c_ref, m_ref, qraw_ref, kraw_ref, qs_ref, ks_ref, n_chunks=n_chunks)


def _mixer_heads(proj, gates_t, log_gamma, cos2, sin2, ret_head_norm, conv_w, conv_b, igate_b, fgate_b,
                 mlstm_head_norm, batch, seq):
    m = proj.shape[0]
    tr = _pick_tile(seq, 512, CHUNK)
    tiles = seq // tr
    base = 2 * RET_QK_W + 2 * RET_V_W
    assert RET_V_W == 2 * RET_QK_W and base % MLSTM_V_W == 0 and MLSTM_V_W == 2 * MLSTM_QK_W
    q_blk = base // MLSTM_QK_W
    v_blk = (base + 2 * MLSTM_QK_W) // MLSTM_V_W
    smem = pl.BlockSpec(memory_space=pltpu.SMEM)
    return pl.pallas_call(
        functools.partial(_mixer_kernel, n_chunks=tr // CHUNK),
        out_shape=(jax.ShapeDtypeStruct((m, RET_V_W), BF16), jax.ShapeDtypeStruct((m, MLSTM_V_W), BF16)),
        grid=(batch, tiles),
        in_specs=[
            smem,
            pl.BlockSpec((tr, RET_QK_W), lambda b, c: (b * tiles + c, 0)),
            pl.BlockSpec((tr, RET_QK_W), lambda b, c: (b * tiles + c, 1)),
            pl.BlockSpec((tr, RET_V_W), lambda b, c: (b * tiles + c, 1)),
            pl.BlockSpec((tr, RET_V_W), lambda b, c: (b * tiles + c, 2)),
            pl.BlockSpec((tr, RET_QK_DIM), lambda b, c: (c, 0)),
            pl.BlockSpec((tr, RET_QK_DIM), lambda b, c: (c, 0)),
            pl.BlockSpec((1, RET_V_W), lambda b, c: (0, 0)),
            smem,
            smem,
            pl.BlockSpec((tr, MLSTM_QK_W), lambda b, c: (b * tiles + c, q_blk)),
            pl.BlockSpec((tr, MLSTM_QK_W), lambda b, c: (b * tiles + c, q_blk + 1)),
            pl.BlockSpec((tr, MLSTM_V_W), lambda b, c: (b * tiles + c, v_blk)),
            pl.BlockSpec((tr, MLSTM_V_W), lambda b, c: (b * tiles + c, v_blk + 1)),
            pl.BlockSpec((SUBLANES, tr), lambda b, c: (0, b * tiles + c)),
            pl.BlockSpec((CONV_WIDTH, MLSTM_QK_W), lambda b, c: (0, 0)),
            pl.BlockSpec((CONV_WIDTH, MLSTM_QK_W), lambda b, c: (0, 1)),
            pl.BlockSpec((1, MLSTM_QK_W), lambda b, c: (0, 0)),
            pl.BlockSpec((1, MLSTM_QK_W), lambda b, c: (0, 1)),
            pl.BlockSpec((1, MLSTM_V_W), lambda b, c: (0, 0)),
        ],
        out_specs=(
            pl.BlockSpec((tr, RET_V_W), lambda b, c: (b * tiles + c, 0)),
            pl.BlockSpec((tr, MLSTM_V_W), lambda b, c: (b * tiles + c, 0)),
        ),
        scratch_shapes=[
            pltpu.VMEM((RET_HEADS, RET_QK_DIM, RET_V_DIM), F32),
            pltpu.VMEM((MLSTM_HEADS, MLSTM_QK_DIM, MLSTM_V_DIM + AUG_W), F32),
            pltpu.VMEM((MLSTM_HEADS, 1, LANES), F32),
            pltpu.VMEM((tr + CONV_HIST, MLSTM_QK_W), F32),
            pltpu.VMEM((tr + CONV_HIST, MLSTM_QK_W), F32),
            pltpu.VMEM((tr, MLSTM_QK_W), BF16),
            pltpu.VMEM((tr, MLSTM_QK_W), BF16),
        ],
        compiler_params=pltpu.CompilerParams(
            dimension_semantics=("parallel", "arbitrary"), vmem_limit_bytes=V7X_VMEM_LIMIT_BYTES
        ),
        name="mixer_heads",
    )(log_gamma, proj, proj, proj, proj, cos2, sin2, ret_head_norm.reshape(1, RET_V_W),
      igate_b, fgate_b, proj, proj, proj, proj, gates_t, conv_w, conv_w, conv_b.reshape(1, -1),
      conv_b.reshape(1, -1), mlstm_head_norm.reshape(1, MLSTM_V_W))


def _outproj_kernel(x_ref, r_ref, h_ref, wr_ref, wh_ref, o_ref):
    acc = jnp.dot(r_ref[...], wr_ref[...], preferred_element_type=F32)
    acc = acc + jnp.dot(h_ref[...], wh_ref[...], preferred_element_type=F32)
    o_ref[...] = x_ref[...] + acc


def _out_proj(x2d, ret, mh, w_out):
    m, d = x2d.shape
    tm = _pick_tile(m, 1024, SUBLANES)
    tn = _pick_tile(d, 1024, LANES)
    wo = w_out.astype(BF16)
    return pl.pallas_call(
        _outproj_kernel,
        out_shape=jax.ShapeDtypeStruct((m, d), F32),
        grid=(m // tm, d // tn),
        in_specs=[
            pl.BlockSpec((tm, tn), lambda i, n: (i, n)),
            pl.BlockSpec((tm, RET_V_W), lambda i, n: (i, 0)),
            pl.BlockSpec((tm, MLSTM_V_W), lambda i, n: (i, 0)),
            pl.BlockSpec((RET_V_W, tn), lambda i, n: (0, n)),
            pl.BlockSpec((MLSTM_V_W, tn), lambda i, n: (1, n)),
        ],
        out_specs=pl.BlockSpec((tm, tn), lambda i, n: (i, n)),
        compiler_params=pltpu.CompilerParams(
            dimension_semantics=("parallel", "parallel"), vmem_limit_bytes=V7X_VMEM_LIMIT_BYTES
        ),
        name="out_proj",
    )(x2d, ret, mh, wo, wo)


def _rotary_tables(seq):
    half = RET_QK_DIM // 2
    inv_freq = ROPE_BASE ** (-jnp.arange(half, dtype=F32) / half)
    ang = jnp.arange(seq, dtype=F32)[:, None] * inv_freq[None, :]
    cos, sin = jnp.cos(ang), jnp.sin(ang)
    return jnp.concatenate([cos, cos], axis=1), jnp.concatenate([-sin, sin], axis=1)


def kernel(x, ffn1_norm, ffn1_w_gate, ffn1_w_up, ffn1_w_down, mix_norm, w_in, conv_w, conv_b, igate_b, fgate_b,
           ret_head_norm, mlstm_head_norm, w_out, ffn2_norm, ffn2_w_gate, ffn2_w_up, ffn2_w_down, final_norm):
    batch, seq, d = x.shape
    depth = ffn1_norm.shape[0]
    assert seq % CHUNK == 0 and w_in.shape[-1] == PROJ_W + N_GATES and w_out.shape[1] == RET_V_W + MLSTM_V_W
    log_gamma = jnp.log1p(-jnp.exp2(-5.0 - jnp.arange(RET_HEADS, dtype=F32)))
    cos2, sin2 = _rotary_tables(seq)
    h = x.reshape(batch * seq, d)
    for layer in range(depth):
        last = layer == depth - 1
        h, xn = _ffn(h, ffn1_norm[layer], ffn1_w_gate[layer], ffn1_w_up[layer], ffn1_w_down[layer], mix_norm[layer],
                     epilogue="emit")
        proj, gates_t = _in_proj(xn, w_in[layer])
        ret, mh = _mixer_heads(proj, gates_t, log_gamma, cos2, sin2, ret_head_norm[layer], conv_w[layer],
                               conv_b[layer], igate_b[layer], fgate_b[layer], mlstm_head_norm[layer], batch, seq)
        h = _out_proj(h, ret, mh, w_out[layer])
        h = _ffn(h, ffn2_norm[layer], ffn2_w_gate[layer], ffn2_w_up[layer], ffn2_w_down[layer], final_norm,
                 epilogue="final" if last else "none")
    return h.reshape(batch, seq, d)
```

```python
import functools

import jax
import jax.numpy as jnp
from jax import lax
from jax.experimental import pallas as pl
from jax.experimental.pallas import tpu as pltpu

F32 = jnp.float32
BF16 = jnp.bfloat16

RET_HEADS = 8
RET_QK_DIM = 128
RET_V_DIM = 256
MLSTM_HEADS = 4
MLSTM_QK_DIM = 256
MLSTM_V_DIM = 512
CONV_WIDTH = 4
CHUNK = 128
ROPE_BASE = 10000.0
NORM_EPS = 1e-6
FFN_RES_WEIGHT = 0.5

RET_QK_W = RET_HEADS * RET_QK_DIM
RET_V_W = RET_HEADS * RET_V_DIM
MLSTM_QK_W = MLSTM_HEADS * MLSTM_QK_DIM
MLSTM_V_W = MLSTM_HEADS * MLSTM_V_DIM
PROJ_W = 2 * RET_QK_W + 2 * RET_V_W + 2 * MLSTM_QK_W + 2 * MLSTM_V_W
N_GATES = 2 * MLSTM_HEADS

LANES = 128
SUBLANES = 8
V7X_VMEM_LIMIT_BYTES = 58 * 1024 * 1024

AUG_W = LANES


def _pick_tile(total, preferred, quantum):
    if total <= preferred:
        return total
    t = (preferred // quantum) * quantum
    while t >= quantum:
        if total % t == 0:
            return t
        t -= quantum
    return total


def _rms_scale(x):
    return lax.rsqrt(jnp.mean(x * x, axis=-1, keepdims=True) + NORM_EPS)


FFN_NORM_ROWS = 128


def _ffn_kernel(x_hbm, g_ref, wg_ref, wu_ref, wd_ref, eg_ref, *rest, tm, tf, epilogue):
    if epilogue == "emit":
        o_hbm, n_hbm, acc_ref, xn_ref, sem = rest
    else:
        o_hbm, acc_ref, xn_ref, sem = rest
    i = pl.program_id(0)
    f = pl.program_id(1)
    n_tiles = pl.num_programs(0)
    last_f = pl.num_programs(1) - 1

    n_slabs = tm // FFN_NORM_ROWS

    def slab(r):
        return pl.ds(pl.multiple_of(r * FFN_NORM_ROWS, FFN_NORM_ROWS), FFN_NORM_ROWS)

    def hbm_rows(tile, r):
        return pl.ds(pl.multiple_of(tile * tm + r * FFN_NORM_ROWS, FFN_NORM_ROWS), FFN_NORM_ROWS)

    def x_copy(tile, r):
        return pltpu.make_async_copy(x_hbm.at[hbm_rows(tile, r)], acc_ref.at[slab(r)], sem.at[0, r])

    def y_copy(tile, r):
        return pltpu.make_async_copy(acc_ref.at[slab(r)], o_hbm.at[hbm_rows(tile, r)], sem.at[1, r])

    def n_copy(tile, r):
        return pltpu.make_async_copy(xn_ref.at[slab(r)], n_hbm.at[hbm_rows(tile, r)], sem.at[2, r])

    @pl.when(f == 0)
    def _():
        def load_rows(r, carry):
            @pl.when(i > 0)
            def _():
                y_copy(i - 1, r).wait()

            x_copy(i, r).start()
            return carry

        lax.fori_loop(0, n_slabs, load_rows, 0)

        def norm_rows(r, carry):
            x_copy(i, r).wait()
            if epilogue == "emit":
                @pl.when(i > 0)
                def _():
                    n_copy(i - 1, r).wait()

            x = acc_ref[slab(r), :]
            xn_ref[slab(r), :] = (x * _rms_scale(x) * g_ref[...]).astype(BF16)
            return carry

        lax.fori_loop(0, n_slabs, norm_rows, 0)

    wgu = jnp.concatenate([wg_ref[...].astype(BF16), wu_ref[...].astype(BF16)], axis=1)
    gu = jnp.dot(xn_ref[...], wgu, preferred_element_type=F32)
    g = gu[:, :tf]
    u = gu[:, tf:]
    h = (g * jax.nn.sigmoid(g) * u * FFN_RES_WEIGHT).astype(BF16)
    acc_ref[...] += jnp.dot(h, wd_ref[...].astype(BF16), preferred_element_type=F32)

    @pl.when(f == last_f)
    def _():
        def finish_rows(r, carry):
            if epilogue == "final":
                y = acc_ref[slab(r), :]
                acc_ref[slab(r), :] = y * _rms_scale(y) * eg_ref[...]
            elif epilogue == "emit":
                y = acc_ref[slab(r), :]
                xn_ref[slab(r), :] = (y * _rms_scale(y) * eg_ref[...]).astype(BF16)
                n_copy(i, r).start()
            y_copy(i, r).start()
            return carry

        lax.fori_loop(0, n_slabs, finish_rows, 0)

        @pl.when(i == n_tiles - 1)
        def _():
            def drain_rows(r, carry):
                y_copy(i, r).wait()
                if epilogue == "emit":
                    n_copy(i, r).wait()
                return carry

            lax.fori_loop(0, n_slabs, drain_rows, 0)


def _ffn(x2d, norm_g, w_gate, w_up, w_down, epilogue_g, *, epilogue):
    m, d = x2d.shape
    dff = w_gate.shape[1]
    tm = _pick_tile(m, 1024, FFN_NORM_ROWS)
    tf = _pick_tile(dff, 256, LANES)
    assert tm % FFN_NORM_ROWS == 0 and epilogue in ("none", "final", "emit")
    hbm = pl.BlockSpec(memory_space=pl.ANY)
    y_shape = jax.ShapeDtypeStruct((m, d), F32)
    emit = epilogue == "emit"
    return pl.pallas_call(
        functools.partial(_ffn_kernel, tm=tm, tf=tf, epilogue=epilogue),
        out_shape=(y_shape, jax.ShapeDtypeStruct((m, d), BF16)) if emit else y_shape,
        grid=(m // tm, dff // tf),
        in_specs=[
            hbm,
            pl.BlockSpec((1, d), lambda i, f: (0, 0)),
            pl.BlockSpec((d, tf), lambda i, f: (0, f)),
            pl.BlockSpec((d, tf), lambda i, f: (0, f)),
            pl.BlockSpec((tf, d), lambda i, f: (f, 0)),
            pl.BlockSpec((1, d), lambda i, f: (0, 0)),
        ],
        out_specs=(hbm, hbm) if emit else hbm,
        scratch_shapes=[
            pltpu.VMEM((tm, d), F32),
            pltpu.VMEM((tm, d), BF16),
            pltpu.SemaphoreType.DMA((3, tm // FFN_NORM_ROWS)),
        ],
        compiler_params=pltpu.CompilerParams(
            dimension_semantics=("arbitrary", "arbitrary"), vmem_limit_bytes=V7X_VMEM_LIMIT_BYTES
        ),
        name="ffn_" + epilogue,
    )(x2d, norm_g.reshape(1, d), w_gate, w_up, w_down, epilogue_g.reshape(1, d))


def _inproj_kernel(xn_ref, wt_ref, wgt_ref, p_ref, gt_ref):
    n = pl.program_id(1)
    trans_b = (((1,), (1,)), ((), ()))

    @pl.when(n == 0)
    def _():
        wg = jnp.concatenate([wgt_ref[...], jnp.zeros_like(wgt_ref)], axis=0).astype(BF16)
        gt = lax.dot_general(wg, xn_ref[...], trans_b, preferred_element_type=F32)
        gt_ref[...] = gt[:SUBLANES, :]

    w = wt_ref[...].astype(BF16)
    p_ref[...] = lax.dot_general(xn_ref[...], w, trans_b, preferred_element_type=F32).astype(BF16)


def _in_proj(xn, w_in):
    m, d = xn.shape
    assert N_GATES == SUBLANES and PROJ_W % SUBLANES == 0
    tm = _pick_tile(m, 1024, LANES)
    tn = _pick_tile(PROJ_W, 768, 2 * LANES)
    w_t = w_in.T
    return pl.pallas_call(
        _inproj_kernel,
        out_shape=(
            jax.ShapeDtypeStruct((m, PROJ_W), BF16),
            jax.ShapeDtypeStruct((SUBLANES, m), F32),
        ),
        grid=(m // tm, PROJ_W // tn),
        in_specs=[
            pl.BlockSpec((tm, d), lambda i, n: (i, 0)),
            pl.BlockSpec((tn, d), lambda i, n: (n, 0)),
            pl.BlockSpec((N_GATES, d), lambda i, n: (PROJ_W // N_GATES, 0)),
        ],
        out_specs=(
            pl.BlockSpec((tm, tn), lambda i, n: (i, n)),
            pl.BlockSpec((SUBLANES, tm), lambda i, n: (0, i)),
        ),
        compiler_params=pltpu.CompilerParams(
            dimension_semantics=("parallel", "arbitrary"), vmem_limit_bytes=V7X_VMEM_LIMIT_BYTES
        ),
        name="in_proj",
    )(xn, w_t, w_t)


CONV_HIST = SUBLANES
HEAD_SPLITS = 2


def _log_sigmoid(x):
    return jnp.minimum(x, 0.0) - jnp.log1p(jnp.exp(-jnp.abs(x)))


def _retention_piece(lg_ref, q_ref, k_ref, v_ref, g_ref, cos_ref, sin_ref, hn_ref, rm_ref, state_ref, rows, heads):
    row = lax.broadcasted_iota(jnp.int32, (CHUNK, CHUNK), 0)
    col = lax.broadcasted_iota(jnp.int32, (CHUNK, CHUNK), 1)
    rel = (row - col).astype(F32)
    pos = lax.broadcasted_iota(jnp.int32, (CHUNK, 1), 0).astype(F32)
    k_scale = RET_QK_DIM ** -0.5
    trans_b = (((1,), (1,)), ((), ()))
    cos = cos_ref[rows, :]
    sin = sin_ref[rows, :]
    for h in heads:
        lg = lg_ref[h]
        intra = jnp.where(rel >= 0, jnp.exp(lg * rel), 0.0)
        q_decay = jnp.exp(lg * (pos + 1.0))
        k_decay = jnp.exp(lg * (CHUNK - 1.0 - pos))
        chunk_decay = jnp.exp(lg * CHUNK)
        qk_cols = pl.ds(h * RET_QK_DIM, RET_QK_DIM)
        v_cols = pl.ds(h * RET_V_DIM, RET_V_DIM)
        q = q_ref[rows, qk_cols].astype(F32)
        k = k_ref[rows, qk_cols].astype(F32)
        q = q * cos + pltpu.roll(q, RET_QK_DIM // 2, 1) * sin
        k = (k * cos + pltpu.roll(k, RET_QK_DIM // 2, 1) * sin) * k_scale
        qb = q.astype(BF16)
        v = v_ref[rows, v_cols]
        state = state_ref[h]
        scores = lax.dot_general(qb, k.astype(BF16), trans_b, preferred_element_type=F32) * intra
        out = jnp.dot(scores.astype(BF16), v, preferred_element_type=F32)
        out = out + jnp.dot(qb, state.astype(BF16), preferred_element_type=F32) * q_decay
        kd_t = (k * k_decay).T.astype(BF16)
        state_ref[h] = state * chunk_decay + jnp.dot(kd_t, v, preferred_element_type=F32)
        gate = g_ref[rows, v_cols].astype(F32)
        y = out * _rms_scale(out) * hn_ref[:, v_cols] * (gate * jax.nn.sigmoid(gate))
        rm_ref[rows, v_cols] = y.astype(BF16)


def _mlstm_piece(ib_ref, fb_ref, v_ref, og_ref, gt_ref, cwq_ref, cwk_ref, cbq_ref, cbk_ref, hn_ref, rm_ref,
                 c_ref, m_ref, qraw_ref, kraw_ref, qwin_ref, kwin_ref, rows, lanes, win_rows, heads):
    row = lax.broadcasted_iota(jnp.int32, (CHUNK, CHUNK), 0)
    col = lax.broadcasted_iota(jnp.int32, (CHUNK, CHUNK), 1)
    causal = col <= row
    ones_col = (lax.broadcasted_iota(jnp.int32, (CHUNK, AUG_W), 1) == 0).astype(BF16)
    trans_b = (((1,), (1,)), ((), ()))

    sub = lax.broadcasted_iota(jnp.int32, (SUBLANES, CHUNK), 0)
    lane = lax.broadcasted_iota(jnp.int32, (SUBLANES, CHUNK), 1)
    bias = jnp.zeros((SUBLANES, CHUNK), F32)
    for h in range(MLSTM_HEADS):
        bias = jnp.where(sub == h, ib_ref[h], bias)
        bias = jnp.where(sub == MLSTM_HEADS + h, fb_ref[h], bias)
    pre = gt_ref[:, lanes] + bias
    cum = _log_sigmoid(pre)
    shift = 1
    while shift < CHUNK:
        cum = cum + jnp.where(lane >= shift, pltpu.roll(cum, shift, 1), 0.0)
        shift *= 2
    g_rows = jnp.where(sub < MLSTM_HEADS, pre, cum)
    g_cols = jnp.concatenate([g_rows, jnp.zeros((CHUNK - SUBLANES, CHUNK), F32)], axis=0).T

    def conv_silu(raw_ref, win_ref, w_ref, b_ref, cols, scale):
        win_ref[...] = raw_ref[win_rows, cols]
        acc = jnp.broadcast_to(b_ref[:, cols], (CHUNK, MLSTM_QK_DIM))
        for j in range(CONV_WIDTH):
            taps = win_ref[pl.ds(CONV_HIST - (CONV_WIDTH - 1) + j, CHUNK), :]
            acc = acc + taps * w_ref[pl.ds(j, 1), cols]
        return (acc * jax.nn.sigmoid(acc) * scale).astype(BF16)

    for h in heads:
        qk_cols = pl.ds(h * MLSTM_QK_DIM, MLSTM_QK_DIM)
        v_cols = pl.ds(h * MLSTM_V_DIM, MLSTM_V_DIM)
        q = conv_silu(qraw_ref, qwin_ref, cwq_ref, cbq_ref, qk_cols, 1.0)
        k = conv_silu(kraw_ref, kwin_ref, cwk_ref, cbk_ref, qk_cols, MLSTM_QK_DIM ** -0.5)
        i_row = g_rows[h:h + 1, :]
        b_row = g_rows[MLSTM_HEADS + h:MLSTM_HEADS + h + 1, :]
        i_col = g_cols[:, h:h + 1]
        b_col = g_cols[:, MLSTM_HEADS + h:MLSTM_HEADS + h + 1]

        m_prev = m_ref[h, :, 0:1]
        d_log = jnp.where(causal, b_col - b_row + i_row, -jnp.inf)
        inter = b_col + m_prev
        m_t = jnp.maximum(inter, jnp.max(d_log, axis=1, keepdims=True))
        w_inter = jnp.exp(inter - m_t)
        w_intra = jnp.exp(d_log - m_t)

        v_aug = jnp.concatenate([v_ref[rows, v_cols], ones_col], axis=1)
        state = c_ref[h]
        s = lax.dot_general(q, k, trans_b, preferred_element_type=F32) * w_intra
        tot = jnp.dot(s.astype(BF16), v_aug, preferred_element_type=F32)
        tot = tot + w_inter * jnp.dot(q, state.astype(BF16), preferred_element_type=F32)
        num = tot[:, :MLSTM_V_DIM]
        nq = tot[:, MLSTM_V_DIM:MLSTM_V_DIM + 1]
        hid = num / jnp.maximum(jnp.abs(nq), jnp.exp(-m_t))

        b_last = b_row[:, CHUNK - 1:CHUNK]
        m_new = m_t[CHUNK - 1:CHUNK, :]
        w_state = jnp.exp(b_last + m_prev - m_new)
        w_key = jnp.exp(b_last - b_col + i_col - m_new)
        kw_t = (k.astype(F32) * w_key).T.astype(BF16)
        c_ref[h] = w_state * state + jnp.dot(kw_t, v_aug, preferred_element_type=F32)
        m_ref[h] = jnp.broadcast_to(m_new, (1, LANES))

        og = og_ref[rows, v_cols].astype(F32)
        y = hid * _rms_scale(hid) * hn_ref[:, v_cols] * jax.nn.sigmoid(og)
        rm_ref[rows, pl.ds(RET_V_W + h * MLSTM_V_DIM, MLSTM_V_DIM)] = y.astype(BF16)


def _mixer_out_kernel(lg_ref, rq_ref, rk_ref, rv_ref, rg_ref, cos_ref, sin_ref, rhn_ref,
                      ib_ref, fb_ref, mq_ref, mk_ref, mv_ref, mo_ref, gt_ref, cwq_ref, cwk_ref, cbq_ref, cbk_ref,
                      mhn_ref, x_ref, w_ref, o_ref,
                      rm0_ref, rm1_ref, state_ref, c_ref, m_ref, qraw_ref, kraw_ref, qwin_ref, kwin_ref,
                      *, tr, tiles_per_seq):
    t = pl.program_id(0)
    n = pl.program_id(1)
    ci = n // HEAD_SPLITS
    rows = pl.ds(pl.multiple_of(ci * CHUNK, CHUNK), CHUNK)
    win_rows = pl.ds(pl.multiple_of(ci * CHUNK, SUBLANES), CHUNK + CONV_HIST)

    @pl.when(n == 0)
    def _():
        @pl.when(t == 0)
        def _():
            rm1_ref[...] = jnp.zeros_like(rm1_ref)

        @pl.when(t % tiles_per_seq == 0)
        def _():
            state_ref[...] = jnp.zeros_like(state_ref)
            c_ref[...] = jnp.zeros_like(c_ref)
            m_ref[...] = jnp.zeros_like(m_ref)
            qraw_ref[pl.ds(0, CONV_HIST), :] = jnp.zeros((CONV_HIST, MLSTM_QK_W), F32)
            kraw_ref[pl.ds(0, CONV_HIST), :] = jnp.zeros((CONV_HIST, MLSTM_QK_W), F32)

        @pl.when(t % tiles_per_seq != 0)
        def _():
            qraw_ref[pl.ds(0, CONV_HIST), :] = qraw_ref[pl.ds(tr, CONV_HIST), :]
            kraw_ref[pl.ds(0, CONV_HIST), :] = kraw_ref[pl.ds(tr, CONV_HIST), :]

        qraw_ref[pl.ds(CONV_HIST, tr), :] = mq_ref[...].astype(F32)
        kraw_ref[pl.ds(CONV_HIST, tr), :] = mk_ref[...].astype(F32)

    def step(split, rm_cur, rm_prev):
        ret_heads = range(split * RET_HEADS // HEAD_SPLITS, (split + 1) * RET_HEADS // HEAD_SPLITS)
        mlstm_heads = range(split * MLSTM_HEADS // HEAD_SPLITS, (split + 1) * MLSTM_HEADS // HEAD_SPLITS)
        acc = jnp.dot(rm_prev[...], w_ref[...], preferred_element_type=F32)
        o_ref[...] = x_ref[...] + acc
        _retention_piece(lg_ref, rq_ref, rk_ref, rv_ref, rg_ref, cos_ref, sin_ref, rhn_ref, rm_cur, state_ref,
                         rows, ret_heads)
        _mlstm_piece(ib_ref, fb_ref, mv_ref, mo_ref, gt_ref, cwq_ref, cwk_ref, cbq_ref, cbk_ref, mhn_ref, rm_cur,
                     c_ref, m_ref, qraw_ref, kraw_ref, qwin_ref, kwin_ref, rows, rows, win_rows, mlstm_heads)

    for split in range(HEAD_SPLITS):
        for parity, (rm_cur, rm_prev) in enumerate(((rm0_ref, rm1_ref), (rm1_ref, rm0_ref))):
            pl.when((n % HEAD_SPLITS == split) & (t % 2 == parity))(functools.partial(step, split, rm_cur, rm_prev))


def _mixer_out(x2d, proj, gates_t, log_gamma, cos2, sin2, ret_head_norm, conv_w, conv_b, igate_b, fgate_b,
               mlstm_head_norm, w_out, batch, seq):
    m, d = x2d.shape
    tr = _pick_tile(seq, 512, CHUNK)
    tiles_per_seq = seq // tr
    n_tiles = batch * tiles_per_seq
    n_cols = HEAD_SPLITS * (tr // CHUNK)
    assert d % (n_cols * LANES) == 0 and RET_HEADS % HEAD_SPLITS == 0 and MLSTM_HEADS % HEAD_SPLITS == 0
    tn = d // n_cols
    base = 2 * RET_QK_W + 2 * RET_V_W
    assert RET_V_W == 2 * RET_QK_W and base % MLSTM_V_W == 0 and MLSTM_V_W == 2 * MLSTM_QK_W
    q_blk = base // MLSTM_QK_W
    v_blk = (base + 2 * MLSTM_QK_W) // MLSTM_V_W
    smem = pl.BlockSpec(memory_space=pltpu.SMEM)

    def cur(t):
        return jnp.minimum(t, n_tiles - 1)

    def prev(t):
        return jnp.maximum(t - 1, 0)

    return pl.pallas_call(
        functools.partial(_mixer_out_kernel, tr=tr, tiles_per_seq=tiles_per_seq),
        out_shape=jax.ShapeDtypeStruct((m, d), F32),
        grid=(n_tiles + 1, n_cols),
        in_specs=[
            smem,
            pl.BlockSpec((tr, RET_QK_W), lambda t, n: (cur(t), 0)),
            pl.BlockSpec((tr, RET_QK_W), lambda t, n: (cur(t), 1)),
            pl.BlockSpec((tr, RET_V_W), lambda t, n: (cur(t), 1)),
            pl.BlockSpec((tr, RET_V_W), lambda t, n: (cur(t), 2)),
            pl.BlockSpec((tr, RET_QK_DIM), lambda t, n: (cur(t) % tiles_per_seq, 0)),
            pl.BlockSpec((tr, RET_QK_DIM), lambda t, n: (cur(t) % tiles_per_seq, 0)),
            pl.BlockSpec((1, RET_V_W), lambda t, n: (0, 0)),
            smem,
            smem,
            pl.BlockSpec((tr, MLSTM_QK_W), lambda t, n: (cur(t), q_blk)),
            pl.BlockSpec((tr, MLSTM_QK_W), lambda t, n: (cur(t), q_blk + 1)),
            pl.BlockSpec((tr, MLSTM_V_W), lambda t, n: (cur(t), v_blk)),
            pl.BlockSpec((tr, MLSTM_V_W), lambda t, n: (cur(t), v_blk + 1)),
            pl.BlockSpec((SUBLANES, tr), lambda t, n: (0, cur(t))),
            pl.BlockSpec((CONV_WIDTH, MLSTM_QK_W), lambda t, n: (0, 0)),
            pl.BlockSpec((CONV_WIDTH, MLSTM_QK_W), lambda t, n: (0, 1)),
            pl.BlockSpec((1, MLSTM_QK_W), lambda t, n: (0, 0)),
            pl.BlockSpec((1, MLSTM_QK_W), lambda t, n: (0, 1)),
            pl.BlockSpec((1, MLSTM_V_W), lambda t, n: (0, 0)),
            pl.BlockSpec((tr, tn), lambda t, n: (prev(t), n)),
            pl.BlockSpec((RET_V_W + MLSTM_V_W, tn), lambda t, n: (0, n)),
        ],
        out_specs=pl.BlockSpec((tr, tn), lambda t, n: (prev(t), jnp.where(t == 0, 0, n))),
        scratch_shapes=[
            pltpu.VMEM((tr, RET_V_W + MLSTM_V_W), BF16),
            pltpu.VMEM((tr, RET_V_W + MLSTM_V_W), BF16),
            pltpu.VMEM((RET_HEADS, RET_QK_DIM, RET_V_DIM), F32),
            pltpu.VMEM((MLSTM_HEADS, MLSTM_QK_DIM, MLSTM_V_DIM + AUG_W), F32),
            pltpu.VMEM((MLSTM_HEADS, 1, LANES), F32),
            pltpu.VMEM((tr + CONV_HIST, MLSTM_QK_W), F32),
            pltpu.VMEM((tr + CONV_HIST, MLSTM_QK_W), F32),
            pltpu.VMEM((CHUNK + CONV_HIST, MLSTM_QK_DIM), F32),
            pltpu.VMEM((CHUNK + CONV_HIST, MLSTM_QK_DIM), F32),
        ],
        compiler_params=pltpu.CompilerParams(
            dimension_semantics=("arbitrary", "arbitrary"), vmem_limit_bytes=V7X_VMEM_LIMIT_BYTES
        ),
        name="mixer_out",
    )(log_gamma, proj, proj, proj, proj, cos2, sin2, ret_head_norm.reshape(1, RET_V_W),
      igate_b, fgate_b, proj, proj, proj, proj, gates_t, conv_w, conv_w, conv_b.reshape(1, -1),
      conv_b.reshape(1, -1), mlstm_head_norm.reshape(1, MLSTM_V_W), x2d, w_out.astype(BF16))


def _rotary_tables(seq):
    half = RET_QK_DIM // 2
    inv_freq = ROPE_BASE ** (-jnp.arange(half, dtype=F32) / half)
    ang = jnp.arange(seq, dtype=F32)[:, None] * inv_freq[None, :]
    cos, sin = jnp.cos(ang), jnp.sin(ang)
    return jnp.concatenate([cos, cos], axis=1), jnp.concatenate([-sin, sin], axis=1)


def kernel(x, ffn1_norm, ffn1_w_gate, ffn1_w_up, ffn1_w_down, mix_norm, w_in, conv_w, conv_b, igate_b, fgate_b,
           ret_head_norm, mlstm_head_norm, w_out, ffn2_norm, ffn2_w_gate, ffn2_w_up, ffn2_w_down, final_norm):
    batch, seq, d = x.shape
    depth = ffn1_norm.shape[0]
    assert seq % CHUNK == 0 and w_in.shape[-1] == PROJ_W + N_GATES and w_out.shape[1] == RET_V_W + MLSTM_V_W
    log_gamma = jnp.log1p(-jnp.exp2(-5.0 - jnp.arange(RET_HEADS, dtype=F32)))
    cos2, sin2 = _rotary_tables(seq)
    h = x.reshape(batch * seq, d)
    for layer in range(depth):
        last = layer == depth - 1
        h, xn = _ffn(h, ffn1_norm[layer], ffn1_w_gate[layer], ffn1_w_up[layer], ffn1_w_down[layer], mix_norm[layer],
                     epilogue="emit")
        proj, gates_t = _in_proj(xn, w_in[layer])
        h = _mixer_out(h, proj, gates_t, log_gamma, cos2, sin2, ret_head_norm[layer], conv_w[layer], conv_b[layer],
                       igate_b[layer], fgate_b[layer], mlstm_head_norm[layer], w_out[layer], batch, seq)
        h = _ffn(h, ffn2_norm[layer], ffn2_w_gate[layer], ffn2_w_up[layer], ffn2_w_down[layer], final_norm,
                 epilogue="final" if last else "none")
    return h.reshape(batch, seq, d)
```

```python
import functools

import jax
import jax.numpy as jnp
from jax import lax
from jax.experimental import pallas as pl
from jax.experimental.pallas import tpu as pltpu

F32 = jnp.float32
BF16 = jnp.bfloat16

RET_HEADS = 8
RET_QK_DIM = 128
RET_V_DIM = 256
MLSTM_HEADS = 4
MLSTM_QK_DIM = 256
MLSTM_V_DIM = 512
CONV_WIDTH = 4
CHUNK = 128
ROPE_BASE = 10000.0
NORM_EPS = 1e-6
FFN_RES_WEIGHT = 0.5

RET_QK_W = RET_HEADS * RET_QK_DIM
RET_V_W = RET_HEADS * RET_V_DIM
MLSTM_QK_W = MLSTM_HEADS * MLSTM_QK_DIM
MLSTM_V_W = MLSTM_HEADS * MLSTM_V_DIM
PROJ_W = 2 * RET_QK_W + 2 * RET_V_W + 2 * MLSTM_QK_W + 2 * MLSTM_V_W
N_GATES = 2 * MLSTM_HEADS

LANES = 128
SUBLANES = 8
V7X_VMEM_LIMIT_BYTES = 58 * 1024 * 1024

AUG_W = LANES


def _pick_tile(total, preferred, quantum):
    if total <= preferred:
        return total
    t = (preferred // quantum) * quantum
    while t >= quantum:
        if total % t == 0:
            return t
        t -= quantum
    return total


def _rms_scale(x):
    return lax.rsqrt(jnp.mean(x * x, axis=-1, keepdims=True) + NORM_EPS)


FFN_NORM_ROWS = 128


def _ffn_kernel(x_hbm, g_hbm, wg_ref, wu_ref, wd_ref, eg_hbm, *rest, tm, tf, epilogue):
    if epilogue == "emit":
        o_hbm, n_hbm, acc_ref, xn_ref, g_ref, eg_ref, sem = rest
    else:
        o_hbm, acc_ref, xn_ref, g_ref, eg_ref, sem = rest
    i = pl.program_id(0)
    f = pl.program_id(1)
    n_tiles = pl.num_programs(0)
    last_f = pl.num_programs(1) - 1

    n_slabs = tm // FFN_NORM_ROWS

    def slab(r):
        return pl.ds(pl.multiple_of(r * FFN_NORM_ROWS, FFN_NORM_ROWS), FFN_NORM_ROWS)

    def hbm_rows(tile, r):
        return pl.ds(pl.multiple_of(tile * tm + r * FFN_NORM_ROWS, FFN_NORM_ROWS), FFN_NORM_ROWS)

    def x_copy(tile, r):
        return pltpu.make_async_copy(x_hbm.at[hbm_rows(tile, r)], acc_ref.at[slab(r)], sem.at[0, r])

    def y_copy(tile, r):
        return pltpu.make_async_copy(acc_ref.at[slab(r)], o_hbm.at[hbm_rows(tile, r)], sem.at[1, r])

    def n_copy(tile, r):
        return pltpu.make_async_copy(xn_ref.at[slab(r)], n_hbm.at[hbm_rows(tile, r)], sem.at[2, r])

    @pl.when(f == 0)
    def _():
        @pl.when(i == 0)
        def _():
            copies = (pltpu.make_async_copy(g_hbm, g_ref, sem.at[3, 0]),
                      pltpu.make_async_copy(eg_hbm, eg_ref, sem.at[3, 1]))
            for c in copies:
                c.start()
            for c in copies:
                c.wait()

        def load_rows(r, carry):
            @pl.when(i > 0)
            def _():
                y_copy(i - 1, r).wait()

            x_copy(i, r).start()
            return carry

        lax.fori_loop(0, n_slabs, load_rows, 0)

        def norm_rows(r, carry):
            x_copy(i, r).wait()
            if epilogue == "emit":
                @pl.when(i > 0)
                def _():
                    n_copy(i - 1, r).wait()

            x = acc_ref[slab(r), :]
            xn_ref[slab(r), :] = (x * _rms_scale(x) * g_ref[...]).astype(BF16)
            return carry

        lax.fori_loop(0, n_slabs, norm_rows, 0)

    wgu = jnp.concatenate([wg_ref[...].astype(BF16), wu_ref[...].astype(BF16)], axis=1)
    gu = jnp.dot(xn_ref[...], wgu, preferred_element_type=F32)
    g = gu[:, :tf]
    u = gu[:, tf:]
    h = (g * jax.nn.sigmoid(g) * u * FFN_RES_WEIGHT).astype(BF16)
    acc_ref[...] += jnp.dot(h, wd_ref[...].astype(BF16), preferred_element_type=F32)

    @pl.when(f == last_f)
    def _():
        def finish_rows(r, carry):
            if epilogue == "final":
                y = acc_ref[slab(r), :]
                acc_ref[slab(r), :] = y * _rms_scale(y) * eg_ref[...]
            elif epilogue == "emit":
                y = acc_ref[slab(r), :]
                xn_ref[slab(r), :] = (y * _rms_scale(y) * eg_ref[...]).astype(BF16)
                n_copy(i, r).start()
            y_copy(i, r).start()
            return carry

        lax.fori_loop(0, n_slabs, finish_rows, 0)

        @pl.when(i == n_tiles - 1)
        def _():
            def drain_rows(r, carry):
                y_copy(i, r).wait()
                if epilogue == "emit":
                    n_copy(i, r).wait()
                return carry

            lax.fori_loop(0, n_slabs, drain_rows, 0)


def _ffn(x2d, norm_g, w_gate, w_up, w_down, epilogue_g, *, epilogue):
    m, d = x2d.shape
    dff = w_gate.shape[1]
    tm = _pick_tile(m, 1024, FFN_NORM_ROWS)
    tf = _pick_tile(dff, 256, LANES)
    assert tm % FFN_NORM_ROWS == 0 and epilogue in ("none", "final", "emit")
    hbm = pl.BlockSpec(memory_space=pl.ANY)
    y_shape = jax.ShapeDtypeStruct((m, d), F32)
    emit = epilogue == "emit"
    return pl.pallas_call(
        functools.partial(_ffn_kernel, tm=tm, tf=tf, epilogue=epilogue),
        out_shape=(y_shape, jax.ShapeDtypeStruct((m, d), BF16)) if emit else y_shape,
        grid=(m // tm, dff // tf),
        in_specs=[
            hbm,
            hbm,
            pl.BlockSpec((d, tf), lambda i, f: (0, f)),
            pl.BlockSpec((d, tf), lambda i, f: (0, f)),
            pl.BlockSpec((tf, d), lambda i, f: (f, 0)),
            hbm,
        ],
        out_specs=(hbm, hbm) if emit else hbm,
        scratch_shapes=[
            pltpu.VMEM((tm, d), F32),
            pltpu.VMEM((tm, d), BF16),
            pltpu.VMEM((1, d), F32),
            pltpu.VMEM((1, d), F32),
            pltpu.SemaphoreType.DMA((4, tm // FFN_NORM_ROWS)),
        ],
        compiler_params=pltpu.CompilerParams(
            dimension_semantics=("arbitrary", "arbitrary"), vmem_limit_bytes=V7X_VMEM_LIMIT_BYTES
        ),
        name="ffn_" + epilogue,
    )(x2d, norm_g.reshape(1, d), w_gate, w_up, w_down, epilogue_g.reshape(1, d))


def _inproj_kernel(xn_ref, wt_ref, wgt_ref, p_ref, gt_ref):
    n = pl.program_id(1)
    trans_b = (((1,), (1,)), ((), ()))

    @pl.when(n == 0)
    def _():
        wg = jnp.concatenate([wgt_ref[...], jnp.zeros_like(wgt_ref)], axis=0).astype(BF16)
        gt = lax.dot_general(wg, xn_ref[...], trans_b, preferred_element_type=F32)
        gt_ref[...] = gt[:SUBLANES, :]

    w = wt_ref[...].astype(BF16)
    p_ref[...] = lax.dot_general(xn_ref[...], w, trans_b, preferred_element_type=F32).astype(BF16)


def _in_proj(xn, w_in):
    m, d = xn.shape
    assert N_GATES == SUBLANES and PROJ_W % SUBLANES == 0
    tm = _pick_tile(m, 1024, LANES)
    tn = _pick_tile(PROJ_W, 768, 2 * LANES)
    w_t = w_in.T
    return pl.pallas_call(
        _inproj_kernel,
        out_shape=(
            jax.ShapeDtypeStruct((m, PROJ_W), BF16),
            jax.ShapeDtypeStruct((SUBLANES, m), F32),
        ),
        grid=(m // tm, PROJ_W // tn),
        in_specs=[
            pl.BlockSpec((tm, d), lambda i, n: (i, 0)),
            pl.BlockSpec((tn, d), lambda i, n: (n, 0)),
            pl.BlockSpec((N_GATES, d), lambda i, n: (PROJ_W // N_GATES, 0)),
        ],
        out_specs=(
            pl.BlockSpec((tm, tn), lambda i, n: (i, n)),
            pl.BlockSpec((SUBLANES, tm), lambda i, n: (0, i)),
        ),
        compiler_params=pltpu.CompilerParams(
            dimension_semantics=("parallel", "arbitrary"), vmem_limit_bytes=V7X_VMEM_LIMIT_BYTES
        ),
        name="in_proj",
    )(xn, w_t, w_t)


def _retention_body(lg_ref, q_ref, k_ref, v_ref, g_ref, cos_ref, sin_ref, hn_ref, o_ref, state_ref, *, n_chunks):
    row = lax.broadcasted_iota(jnp.int32, (CHUNK, CHUNK), 0)
    col = lax.broadcasted_iota(jnp.int32, (CHUNK, CHUNK), 1)
    rel = (row - col).astype(F32)
    pos = lax.broadcasted_iota(jnp.int32, (CHUNK, 1), 0).astype(F32)
    k_scale = RET_QK_DIM ** -0.5
    trans_b = (((1,), (1,)), ((), ()))

    decays = []
    for h in range(RET_HEADS):
        lg = lg_ref[h]
        decays.append((
            jnp.where(rel >= 0, jnp.exp(lg * rel), 0.0),
            jnp.exp(lg * (pos + 1.0)),
            jnp.exp(lg * (CHUNK - 1.0 - pos)),
            jnp.exp(lg * CHUNK),
        ))

    for ci in range(n_chunks):
        rows = pl.ds(ci * CHUNK, CHUNK)
        cos = cos_ref[rows, :]
        sin = sin_ref[rows, :]
        for h in range(RET_HEADS):
            intra, q_decay, k_decay, chunk_decay = decays[h]
            qk_cols = pl.ds(h * RET_QK_DIM, RET_QK_DIM)
            v_cols = pl.ds(h * RET_V_DIM, RET_V_DIM)
            q = q_ref[rows, qk_cols].astype(F32)
            k = k_ref[rows, qk_cols].astype(F32)
            q = q * cos + pltpu.roll(q, RET_QK_DIM // 2, 1) * sin
            k = (k * cos + pltpu.roll(k, RET_QK_DIM // 2, 1) * sin) * k_scale
            qb = q.astype(BF16)
            v = v_ref[rows, v_cols]
            state = state_ref[h]
            scores = lax.dot_general(qb, k.astype(BF16), trans_b, preferred_element_type=F32) * intra
            out = jnp.dot(scores.astype(BF16), v, preferred_element_type=F32)
            out = out + jnp.dot(qb, state.astype(BF16), preferred_element_type=F32) * q_decay
            kd_t = (k * k_decay).T.astype(BF16)
            state_ref[h] = state * chunk_decay + jnp.dot(kd_t, v, preferred_element_type=F32)
            gate = g_ref[rows, v_cols].astype(F32)
            y = out * _rms_scale(out) * hn_ref[:, v_cols] * (gate * jax.nn.sigmoid(gate))
            o_ref[rows, v_cols] = y.astype(BF16)


CONV_HIST = SUBLANES


def _log_sigmoid(x):
    return jnp.minimum(x, 0.0) - jnp.log1p(jnp.exp(-jnp.abs(x)))


def _mlstm_body(ib_ref, fb_ref, q_ref, k_ref, v_ref, og_ref, gt_ref, cwq_ref, cwk_ref, cbq_ref, cbk_ref, hn_ref,
                o_ref, c_ref, m_ref, qraw_ref, kraw_ref, qs_ref, ks_ref, *, n_chunks):
    tr = n_chunks * CHUNK
    hist = CONV_HIST

    def conv_silu(raw_ref, src_ref, w_ref, b_ref, dst_ref, scale):
        raw_ref[pl.ds(hist, tr), :] = src_ref[...].astype(F32)
        for ci in range(n_chunks):
            acc = jnp.broadcast_to(b_ref[...], (CHUNK, MLSTM_QK_W))
            for j in range(CONV_WIDTH):
                taps = raw_ref[pl.ds(ci * CHUNK + hist - (CONV_WIDTH - 1) + j, CHUNK), :]
                acc = acc + taps * w_ref[pl.ds(j, 1), :]
            dst_ref[pl.ds(ci * CHUNK, CHUNK), :] = (acc * jax.nn.sigmoid(acc) * scale).astype(BF16)
        raw_ref[pl.ds(0, hist), :] = raw_ref[pl.ds(tr, hist), :]

    conv_silu(qraw_ref, q_ref, cwq_ref, cbq_ref, qs_ref, 1.0)
    conv_silu(kraw_ref, k_ref, cwk_ref, cbk_ref, ks_ref, MLSTM_QK_DIM ** -0.5)

    row = lax.broadcasted_iota(jnp.int32, (CHUNK, CHUNK), 0)
    col = lax.broadcasted_iota(jnp.int32, (CHUNK, CHUNK), 1)
    causal = col <= row
    ones_col = (lax.broadcasted_iota(jnp.int32, (CHUNK, AUG_W), 1) == 0).astype(BF16)
    trans_b = (((1,), (1,)), ((), ()))

    sub = lax.broadcasted_iota(jnp.int32, (SUBLANES, tr), 0)
    pos = lax.broadcasted_iota(jnp.int32, (SUBLANES, tr), 1) % CHUNK
    bias = jnp.zeros((SUBLANES, tr), F32)
    for h in range(MLSTM_HEADS):
        bias = jnp.where(sub == h, ib_ref[h], bias)
        bias = jnp.where(sub == MLSTM_HEADS + h, fb_ref[h], bias)
    pre = gt_ref[...] + bias
    cum = _log_sigmoid(pre)
    shift = 1
    while shift < CHUNK:
        cum = cum + jnp.where(pos >= shift, pltpu.roll(cum, shift, 1), 0.0)
        shift *= 2
    gate_rows = jnp.where(sub < MLSTM_HEADS, pre, cum)
    pad_rows = jnp.zeros((CHUNK - SUBLANES, CHUNK), F32)

    for ci in range(n_chunks):
        rows = pl.ds(ci * CHUNK, CHUNK)
        g_rows = gate_rows[:, ci * CHUNK:(ci + 1) * CHUNK]
        g_cols = jnp.concatenate([g_rows, pad_rows], axis=0).T
        for h in range(MLSTM_HEADS):
            qk_cols = pl.ds(h * MLSTM_QK_DIM, MLSTM_QK_DIM)
            v_cols = pl.ds(h * MLSTM_V_DIM, MLSTM_V_DIM)
            i_row = g_rows[h:h + 1, :]
            b_row = g_rows[MLSTM_HEADS + h:MLSTM_HEADS + h + 1, :]
            i_col = g_cols[:, h:h + 1]
            b_col = g_cols[:, MLSTM_HEADS + h:MLSTM_HEADS + h + 1]

            m_prev = m_ref[h, :, 0:1]
            d_log = jnp.where(causal, b_col - b_row + i_row, -jnp.inf)
            inter = b_col + m_prev
            m_t = jnp.maximum(inter, jnp.max(d_log, axis=1, keepdims=True))
            w_inter = jnp.exp(inter - m_t)
            w_intra = jnp.exp(d_log - m_t)

            q = qs_ref[rows, qk_cols]
            k = ks_ref[rows, qk_cols]
            v_aug = jnp.concatenate([v_ref[rows, v_cols], ones_col], axis=1)
            state = c_ref[h]
            s = lax.dot_general(q, k, trans_b, preferred_element_type=F32) * w_intra
            tot = jnp.dot(s.astype(BF16), v_aug, preferred_element_type=F32)
            tot = tot + w_inter * jnp.dot(q, state.astype(BF16), preferred_element_type=F32)
            num = tot[:, :MLSTM_V_DIM]
            nq = tot[:, MLSTM_V_DIM:MLSTM_V_DIM + 1]
            hid = num / jnp.maximum(jnp.abs(nq), jnp.exp(-m_t))

            b_last = b_row[:, CHUNK - 1:CHUNK]
            m_new = m_t[CHUNK - 1:CHUNK, :]
            w_state = jnp.exp(b_last + m_prev - m_new)
            w_key = jnp.exp(b_last - b_col + i_col - m_new)
            kw_t = (k.astype(F32) * w_key).T.astype(BF16)
            c_ref[h] = w_state * state + jnp.dot(kw_t, v_aug, preferred_element_type=F32)
            m_ref[h] = jnp.broadcast_to(m_new, (1, LANES))

            og = og_ref[rows, v_cols].astype(F32)
            y = hid * _rms_scale(hid) * hn_ref[:, v_cols] * jax.nn.sigmoid(og)
            o_ref[rows, v_cols] = y.astype(BF16)


N_RET_IN = 8
N_MLSTM_IN = 12


def _mixer_kernel(*refs, n_chunks):
    ret_in = refs[:N_RET_IN]
    mlstm_in = refs[N_RET_IN:N_RET_IN + N_MLSTM_IN]
    ret_o, mh_o, state_ref, c_ref, m_ref, qraw_ref, kraw_ref, qs_ref, ks_ref = refs[N_RET_IN + N_MLSTM_IN:]

    @pl.when(pl.program_id(1) == 0)
    def _():
        state_ref[...] = jnp.zeros_like(state_ref)
        c_ref[...] = jnp.zeros_like(c_ref)
        m_ref[...] = jnp.zeros_like(m_ref)
        qraw_ref[pl.ds(0, CONV_HIST), :] = jnp.zeros((CONV_HIST, MLSTM_QK_W), F32)
        kraw_ref[pl.ds(0, CONV_HIST), :] = jnp.zeros((CONV_HIST, MLSTM_QK_W), F32)

    _retention_body(*ret_in, ret_o, state_ref, n_chunks=n_chunks)
    _mlstm_body(*mlstm_in, mh_o, c_ref, m_ref, qraw_ref, kraw_ref, qs_ref, ks_ref, n_chunks=n_chunks)


def _mixer_heads(proj, gates_t, log_gamma, cos2, sin2, ret_head_norm, conv_w, conv_b, igate_b, fgate_b,
                 mlstm_head_norm, batch, seq):
    m = proj.shape[0]
    tr = _pick_tile(seq, 512, CHUNK)
    tiles = seq // tr
    base = 2 * RET_QK_W + 2 * RET_V_W
    assert RET_V_W == 2 * RET_QK_W and base % MLSTM_V_W == 0 and MLSTM_V_W == 2 * MLSTM_QK_W
    q_blk = base // MLSTM_QK_W
    v_blk = (base + 2 * MLSTM_QK_W) // MLSTM_V_W
    smem = pl.BlockSpec(memory_space=pltpu.SMEM)
    return pl.pallas_call(
        functools.partial(_mixer_kernel, n_chunks=tr // CHUNK),
        out_shape=(jax.ShapeDtypeStruct((m, RET_V_W), BF16), jax.ShapeDtypeStruct((m, MLSTM_V_W), BF16)),
        grid=(batch, tiles),
        in_specs=[
            smem,
            pl.BlockSpec((tr, RET_QK_W), lambda b, c: (b * tiles + c, 0)),
            pl.BlockSpec((tr, RET_QK_W), lambda b, c: (b * tiles + c, 1)),
            pl.BlockSpec((tr, RET_V_W), lambda b, c: (b * tiles + c, 1)),
            pl.BlockSpec((tr, RET_V_W), lambda b, c: (b * tiles + c, 2)),
            pl.BlockSpec((tr, RET_QK_DIM), lambda b, c: (c, 0)),
            pl.BlockSpec((tr, RET_QK_DIM), lambda b, c: (c, 0)),
            pl.BlockSpec((1, RET_V_W), lambda b, c: (0, 0)),
            smem,
            smem,
            pl.BlockSpec((tr, MLSTM_QK_W), lambda b, c: (b * tiles + c, q_blk)),
            pl.BlockSpec((tr, MLSTM_QK_W), lambda b, c: (b * tiles + c, q_blk + 1)),
            pl.BlockSpec((tr, MLSTM_V_W), lambda b, c: (b * tiles + c, v_blk)),
            pl.BlockSpec((tr, MLSTM_V_W), lambda b, c: (b * tiles + c, v_blk + 1)),
            pl.BlockSpec((SUBLANES, tr), lambda b, c: (0, b * tiles + c)),
            pl.BlockSpec((CONV_WIDTH, MLSTM_QK_W), lambda b, c: (0, 0)),
            pl.BlockSpec((CONV_WIDTH, MLSTM_QK_W), lambda b, c: (0, 1)),
            pl.BlockSpec((1, MLSTM_QK_W), lambda b, c: (0, 0)),
            pl.BlockSpec((1, MLSTM_QK_W), lambda b, c: (0, 1)),
            pl.BlockSpec((1, MLSTM_V_W), lambda b, c: (0, 0)),
        ],
        out_specs=(
            pl.BlockSpec((tr, RET_V_W), lambda b, c: (b * tiles + c, 0)),
            pl.BlockSpec((tr, MLSTM_V_W), lambda b, c: (b * tiles + c, 0)),
        ),
        scratch_shapes=[
            pltpu.VMEM((RET_HEADS, RET_QK_DIM, RET_V_DIM), F32),
            pltpu.VMEM((MLSTM_HEADS, MLSTM_QK_DIM, MLSTM_V_DIM + AUG_W), F32),
            pltpu.VMEM((MLSTM_HEADS, 1, LANES), F32),
            pltpu.VMEM((tr + CONV_HIST, MLSTM_QK_W), F32),
            pltpu.VMEM((tr + CONV_HIST, MLSTM_QK_W), F32),
            pltpu.VMEM((tr, MLSTM_QK_W), BF16),
            pltpu.VMEM((tr, MLSTM_QK_W), BF16),
        ],
        compiler_params=pltpu.CompilerParams(
            dimension_semantics=("parallel", "arbitrary"), vmem_limit_bytes=V7X_VMEM_LIMIT_BYTES
        ),
        name="mixer_heads",
    )(log_gamma, proj, proj, proj, proj, cos2, sin2, ret_head_norm.reshape(1, RET_V_W),
      igate_b, fgate_b, proj, proj, proj, proj, gates_t, conv_w, conv_w, conv_b.reshape(1, -1),
      conv_b.reshape(1, -1), mlstm_head_norm.reshape(1, MLSTM_V_W))


def _outproj_kernel(x_ref, r_ref, h_ref, wr_ref, wh_ref, o_ref):
    acc = jnp.dot(r_ref[...], wr_ref[...], preferred_element_type=F32)
    acc = acc + jnp.dot(h_ref[...], wh_ref[...], preferred_element_type=F32)
    o_ref[...] = x_ref[...] + acc


def _out_proj(x2d, ret, mh, w_out):
    m, d = x2d.shape
    tm = _pick_tile(m, 1024, SUBLANES)
    tn = _pick_tile(d, 1024, LANES)
    wo = w_out.astype(BF16)
    return pl.pallas_call(
        _outproj_kernel,
        out_shape=jax.ShapeDtypeStruct((m, d), F32),
        grid=(m // tm, d // tn),
        in_specs=[
            pl.BlockSpec((tm, tn), lambda i, n: (i, n)),
            pl.BlockSpec((tm, RET_V_W), lambda i, n: (i, 0)),
            pl.BlockSpec((tm, MLSTM_V_W), lambda i, n: (i, 0)),
            pl.BlockSpec((RET_V_W, tn), lambda i, n: (0, n)),
            pl.BlockSpec((MLSTM_V_W, tn), lambda i, n: (1, n)),
        ],
        out_specs=pl.BlockSpec((tm, tn), lambda i, n: (i, n)),
        compiler_params=pltpu.CompilerParams(
            dimension_semantics=("parallel", "parallel"), vmem_limit_bytes=V7X_VMEM_LIMIT_BYTES
        ),
        name="out_proj",
    )(x2d, ret, mh, wo, wo)


def _rotary_tables(seq):
    half = RET_QK_DIM // 2
    inv_freq = ROPE_BASE ** (-jnp.arange(half, dtype=F32) / half)
    ang = jnp.arange(seq, dtype=F32)[:, None] * inv_freq[None, :]
    cos, sin = jnp.cos(ang), jnp.sin(ang)
    return jnp.concatenate([cos, cos], axis=1), jnp.concatenate([-sin, sin], axis=1)


def kernel(x, ffn1_norm, ffn1_w_gate, ffn1_w_up, ffn1_w_down, mix_norm, w_in, conv_w, conv_b, igate_b, fgate_b,
           ret_head_norm, mlstm_head_norm, w_out, ffn2_norm, ffn2_w_gate, ffn2_w_up, ffn2_w_down, final_norm):
    batch, seq, d = x.shape
    depth = ffn1_norm.shape[0]
    assert seq % CHUNK == 0 and w_in.shape[-1] == PROJ_W + N_GATES and w_out.shape[1] == RET_V_W + MLSTM_V_W
    log_gamma = jnp.log1p(-jnp.exp2(-5.0 - jnp.arange(RET_HEADS, dtype=F32)))
    cos2, sin2 = _rotary_tables(seq)
    h = x.reshape(batch * seq, d)
    for layer in range(depth):
        last = layer == depth - 1
        h, xn = _ffn(h, ffn1_norm[layer], ffn1_w_gate[layer], ffn1_w_up[layer], ffn1_w_down[layer], mix_norm[layer],
                     epilogue="emit")
        proj, gates_t = _in_proj(xn, w_in[layer])
        ret, mh = _mixer_heads(proj, gates_t, log_gamma, cos2, sin2, ret_head_norm[layer], conv_w[layer],
                               conv_b[layer], igate_b[layer], fgate_b[layer], mlstm_head_norm[layer], batch, seq)
        h = _out_proj(h, ret, mh, w_out[layer])
        h = _ffn(h, ffn2_norm[layer], ffn2_w_gate[layer], ffn2_w_up[layer], ffn2_w_down[layer], final_norm,
                 epilogue="final" if last else "none")
    return h.reshape(batch, seq, d)
```

```python
import functools

import jax
import jax.numpy as jnp
from jax import lax
from jax.experimental import pallas as pl
from jax.experimental.pallas import tpu as pltpu

F32 = jnp.float32
BF16 = jnp.bfloat16

RET_HEADS = 8
RET_QK_DIM = 128
RET_V_DIM = 256
MLSTM_HEADS = 4
MLSTM_QK_DIM = 256
MLSTM_V_DIM = 512
CONV_WIDTH = 4
CHUNK = 128
ROPE_BASE = 10000.0
NORM_EPS = 1e-6
FFN_RES_WEIGHT = 0.5

RET_QK_W = RET_HEADS * RET_QK_DIM
RET_V_W = RET_HEADS * RET_V_DIM
MLSTM_QK_W = MLSTM_HEADS * MLSTM_QK_DIM
MLSTM_V_W = MLSTM_HEADS * MLSTM_V_DIM
PROJ_W = 2 * RET_QK_W + 2 * RET_V_W + 2 * MLSTM_QK_W + 2 * MLSTM_V_W
N_GATES = 2 * MLSTM_HEADS

LANES = 128
SUBLANES = 8
V7X_VMEM_LIMIT_BYTES = 58 * 1024 * 1024

AUG_W = LANES


def _pick_tile(total, preferred, quantum):
    if total <= preferred:
        return total
    t = (preferred // quantum) * quantum
    while t >= quantum:
        if total % t == 0:
            return t
        t -= quantum
    return total


def _rms_scale(x):
    return lax.rsqrt(jnp.mean(x * x, axis=-1, keepdims=True) + NORM_EPS)


FFN_NORM_ROWS = 128


def _ffn_kernel(x_hbm, g_ref, wg_ref, wu_ref, wd_ref, eg_ref, *rest, tm, tf, epilogue):
    if epilogue == "emit":
        o_hbm, n_hbm, acc_ref, xn_ref, sem = rest
    else:
        o_hbm, acc_ref, xn_ref, sem = rest
    i = pl.program_id(0)
    f = pl.program_id(1)
    n_tiles = pl.num_programs(0)
    last_f = pl.num_programs(1) - 1

    n_slabs = tm // FFN_NORM_ROWS

    def slab(r):
        return pl.ds(pl.multiple_of(r * FFN_NORM_ROWS, FFN_NORM_ROWS), FFN_NORM_ROWS)

    def hbm_rows(tile, r):
        return pl.ds(pl.multiple_of(tile * tm + r * FFN_NORM_ROWS, FFN_NORM_ROWS), FFN_NORM_ROWS)

    def x_copy(tile, r):
        return pltpu.make_async_copy(x_hbm.at[hbm_rows(tile, r)], acc_ref.at[slab(r)], sem.at[0, r])

    def y_copy(tile, r):
        return pltpu.make_async_copy(acc_ref.at[slab(r)], o_hbm.at[hbm_rows(tile, r)], sem.at[1, r])

    def n_copy(tile, r):
        return pltpu.make_async_copy(xn_ref.at[slab(r)], n_hbm.at[hbm_rows(tile, r)], sem.at[2, r])

    @pl.when(f == 0)
    def _():
        def load_rows(r, carry):
            @pl.when(i > 0)
            def _():
                y_copy(i - 1, r).wait()

            x_copy(i, r).start()
            return carry

        lax.fori_loop(0, n_slabs, load_rows, 0)

        def norm_rows(r, carry):
            x_copy(i, r).wait()
            if epilogue == "emit":
                @pl.when(i > 0)
                def _():
                    n_copy(i - 1, r).wait()

            x = acc_ref[slab(r), :]
            xn_ref[slab(r), :] = (x * _rms_scale(x) * g_ref[...]).astype(BF16)
            return carry

        lax.fori_loop(0, n_slabs, norm_rows, 0)

    wgu = jnp.concatenate([wg_ref[...].astype(BF16), wu_ref[...].astype(BF16)], axis=1)
    gu = jnp.dot(xn_ref[...], wgu, preferred_element_type=F32)
    g = gu[:, :tf]
    u = gu[:, tf:]
    h = (g * jax.nn.sigmoid(g) * u * FFN_RES_WEIGHT).astype(BF16)
    acc_ref[...] += jnp.dot(h, wd_ref[...].astype(BF16), preferred_element_type=F32)

    @pl.when(f == last_f)
    def _():
        def finish_rows(r, carry):
            if epilogue == "final":
                y = acc_ref[slab(r), :]
                acc_ref[slab(r), :] = y * _rms_scale(y) * eg_ref[...]
            elif epilogue == "emit":
                y = acc_ref[slab(r), :]
                xn_ref[slab(r), :] = (y * _rms_scale(y) * eg_ref[...]).astype(BF16)
                n_copy(i, r).start()
            y_copy(i, r).start()
            return carry

        lax.fori_loop(0, n_slabs, finish_rows, 0)

        @pl.when(i == n_tiles - 1)
        def _():
            def drain_rows(r, carry):
                y_copy(i, r).wait()
                if epilogue == "emit":
                    n_copy(i, r).wait()
                return carry

            lax.fori_loop(0, n_slabs, drain_rows, 0)


def _ffn(x2d, norm_g, w_gate, w_up, w_down, epilogue_g, *, epilogue):
    m, d = x2d.shape
    dff = w_gate.shape[1]
    tm = _pick_tile(m, 1024, FFN_NORM_ROWS)
    tf = _pick_tile(dff, 256, LANES)
    assert tm % FFN_NORM_ROWS == 0 and epilogue in ("none", "final", "emit")
    hbm = pl.BlockSpec(memory_space=pl.ANY)
    y_shape = jax.ShapeDtypeStruct((m, d), F32)
    emit = epilogue == "emit"
    return pl.pallas_call(
        functools.partial(_ffn_kernel, tm=tm, tf=tf, epilogue=epilogue),
        out_shape=(y_shape, jax.ShapeDtypeStruct((m, d), BF16)) if emit else y_shape,
        grid=(m // tm, dff // tf),
        in_specs=[
            hbm,
            pl.BlockSpec((1, d), lambda i, f: (0, 0)),
            pl.BlockSpec((d, tf), lambda i, f: (0, f)),
            pl.BlockSpec((d, tf), lambda i, f: (0, f)),
            pl.BlockSpec((tf, d), lambda i, f: (f, 0)),
            pl.BlockSpec((1, d), lambda i, f: (0, 0)),
        ],
        out_specs=(hbm, hbm) if emit else hbm,
        scratch_shapes=[
            pltpu.VMEM((tm, d), F32),
            pltpu.VMEM((tm, d), BF16),
            pltpu.SemaphoreType.DMA((3, tm // FFN_NORM_ROWS)),
        ],
        compiler_params=pltpu.CompilerParams(
            dimension_semantics=("arbitrary", "arbitrary"), vmem_limit_bytes=V7X_VMEM_LIMIT_BYTES
        ),
        name="ffn_" + epilogue,
    )(x2d, norm_g.reshape(1, d), w_gate, w_up, w_down, epilogue_g.reshape(1, d))


INPROJ_SUB = 2 * LANES


def _inproj_kernel(xn_ref, wt_ref, wgt_ref, wo_ref, p_ref, gt_ref, wo_bf_ref):
    n = pl.program_id(1)
    trans_b = (((1,), (1,)), ((), ()))

    wo_bf_ref[...] = wo_ref[...].astype(BF16)

    @pl.when(n == 0)
    def _():
        wg = jnp.concatenate([wgt_ref[...], jnp.zeros_like(wgt_ref)], axis=0).astype(BF16)
        gt = lax.dot_general(wg, xn_ref[...], trans_b, preferred_element_type=F32)
        gt_ref[...] = gt[:SUBLANES, :]

    for j in range(wt_ref.shape[0] // INPROJ_SUB):
        cols = pl.ds(j * INPROJ_SUB, INPROJ_SUB)
        w = wt_ref[cols, :].astype(BF16)
        p_ref[:, cols] = lax.dot_general(xn_ref[...], w, trans_b, preferred_element_type=F32).astype(BF16)


def _in_proj(xn, w_in, w_out):
    m, d = xn.shape
    assert N_GATES == SUBLANES and PROJ_W % SUBLANES == 0
    tm = _pick_tile(m, 1024, LANES)
    tn = _pick_tile(PROJ_W, 1024, INPROJ_SUB)
    n_cols = PROJ_W // tn
    n_steps = (m // tm) * n_cols
    wo_rows, wo_cols = w_out.shape
    cast_rows = next(r for r in range(2 * SUBLANES, wo_rows + 1, 2 * SUBLANES)
                     if wo_rows % r == 0 and wo_rows // r <= n_steps)
    n_cast = wo_rows // cast_rows

    def cast_block(i, n):
        return (jnp.minimum(i * n_cols + n, n_cast - 1), 0)

    w_t = w_in.T
    return pl.pallas_call(
        _inproj_kernel,
        out_shape=(
            jax.ShapeDtypeStruct((m, PROJ_W), BF16),
            jax.ShapeDtypeStruct((SUBLANES, m), F32),
            jax.ShapeDtypeStruct((wo_rows, wo_cols), BF16),
        ),
        grid=(m // tm, n_cols),
        in_specs=[
            pl.BlockSpec((tm, d), lambda i, n: (i, 0)),
            pl.BlockSpec((tn, d), lambda i, n: (n, 0)),
            pl.BlockSpec((N_GATES, d), lambda i, n: (PROJ_W // N_GATES, 0)),
            pl.BlockSpec((cast_rows, wo_cols), cast_block),
        ],
        out_specs=(
            pl.BlockSpec((tm, tn), lambda i, n: (i, n)),
            pl.BlockSpec((SUBLANES, tm), lambda i, n: (0, i)),
            pl.BlockSpec((cast_rows, wo_cols), cast_block),
        ),
        compiler_params=pltpu.CompilerParams(
            dimension_semantics=("arbitrary", "arbitrary"), vmem_limit_bytes=V7X_VMEM_LIMIT_BYTES
        ),
        name="in_proj",
    )(xn, w_t, w_t, w_out)


def _retention_body(lg_ref, q_ref, k_ref, v_ref, g_ref, cos_ref, sin_ref, hn_ref, o_ref, state_ref, *, n_chunks):
    row = lax.broadcasted_iota(jnp.int32, (CHUNK, CHUNK), 0)
    col = lax.broadcasted_iota(jnp.int32, (CHUNK, CHUNK), 1)
    rel = (row - col).astype(F32)
    pos = lax.broadcasted_iota(jnp.int32, (CHUNK, 1), 0).astype(F32)
    k_scale = RET_QK_DIM ** -0.5
    trans_b = (((1,), (1,)), ((), ()))

    decays = []
    for h in range(RET_HEADS):
        lg = lg_ref[h]
        decays.append((
            jnp.where(rel >= 0, jnp.exp(lg * rel), 0.0),
            jnp.exp(lg * (pos + 1.0)),
            jnp.exp(lg * (CHUNK - 1.0 - pos)),
            jnp.exp(lg * CHUNK),
        ))

    for ci in range(n_chunks):
        rows = pl.ds(ci * CHUNK, CHUNK)
        cos = cos_ref[rows, :]
        sin = sin_ref[rows, :]
        for h in range(RET_HEADS):
            intra, q_decay, k_decay, chunk_decay = decays[h]
            qk_cols = pl.ds(h * RET_QK_DIM, RET_QK_DIM)
            v_cols = pl.ds(h * RET_V_DIM, RET_V_DIM)
            q = q_ref[rows, qk_cols].astype(F32)
            k = k_ref[rows, qk_cols].astype(F32)
            q = q * cos + pltpu.roll(q, RET_QK_DIM // 2, 1) * sin
            k = (k * cos + pltpu.roll(k, RET_QK_DIM // 2, 1) * sin) * k_scale
            qb = q.astype(BF16)
            v = v_ref[rows, v_cols]
            state = state_ref[h]
            scores = lax.dot_general(qb, k.astype(BF16), trans_b, preferred_element_type=F32) * intra
            out = jnp.dot(scores.astype(BF16), v, preferred_element_type=F32)
            out = out + jnp.dot(qb, state.astype(BF16), preferred_element_type=F32) * q_decay
            kd_t = (k * k_decay).T.astype(BF16)
            state_ref[h] = state * chunk_decay + jnp.dot(kd_t, v, preferred_element_type=F32)
            gate = g_ref[rows, v_cols].astype(F32)
            y = out * _rms_scale(out) * hn_ref[:, v_cols] * (gate * jax.nn.sigmoid(gate))
            o_ref[rows, v_cols] = y.astype(BF16)


def _log_sigmoid(x):
    return jnp.minimum(x, 0.0) - jnp.log1p(jnp.exp(-jnp.abs(x)))


def _mlstm_body(ib_ref, fb_ref, q_ref, k_ref, v_ref, og_ref, gt_ref, cwq_ref, cwk_ref, cbq_ref, cbk_ref, hn_ref,
                o_ref, c_ref, m_ref, qprev_ref, kprev_ref, qs_ref, ks_ref, *, n_chunks, after_prepare):
    tr = n_chunks * CHUNK

    sel_row = lax.broadcasted_iota(jnp.int32, (CHUNK, 2 * CHUNK), 0)
    sel_col = lax.broadcasted_iota(jnp.int32, (CHUNK, 2 * CHUNK), 1)
    selectors = [(sel_col == sel_row + CHUNK - (CONV_WIDTH - 1 - j)).astype(BF16) for j in range(CONV_WIDTH - 1)]

    def conv_silu(prev_ref, src_ref, w_ref, b_ref, dst_ref, scale):
        prev = prev_ref[...]
        for ci in range(n_chunks):
            cur = src_ref[pl.ds(ci * CHUNK, CHUNK), :]
            both = jnp.concatenate([prev, cur], axis=0)
            acc = b_ref[...] + cur.astype(F32) * w_ref[pl.ds(CONV_WIDTH - 1, 1), :]
            for j in range(CONV_WIDTH - 1):
                shifted = jnp.dot(selectors[j], both, preferred_element_type=F32)
                acc = acc + shifted * w_ref[pl.ds(j, 1), :]
            dst_ref[pl.ds(ci * CHUNK, CHUNK), :] = (acc * jax.nn.sigmoid(acc) * scale).astype(BF16)
            prev = cur
        prev_ref[...] = prev

    conv_silu(qprev_ref, q_ref, cwq_ref, cbq_ref, qs_ref, 1.0)
    conv_silu(kprev_ref, k_ref, cwk_ref, cbk_ref, ks_ref, MLSTM_QK_DIM ** -0.5)

    row = lax.broadcasted_iota(jnp.int32, (CHUNK, CHUNK), 0)
    col = lax.broadcasted_iota(jnp.int32, (CHUNK, CHUNK), 1)
    causal = col <= row
    ones_col = (lax.broadcasted_iota(jnp.int32, (CHUNK, AUG_W), 1) == 0).astype(BF16)
    trans_b = (((1,), (1,)), ((), ()))

    sub = lax.broadcasted_iota(jnp.int32, (SUBLANES, tr), 0)
    pos = lax.broadcasted_iota(jnp.int32, (SUBLANES, tr), 1) % CHUNK
    bias = jnp.zeros((SUBLANES, tr), F32)
    for h in range(MLSTM_HEADS):
        bias = jnp.where(sub == h, ib_ref[h], bias)
        bias = jnp.where(sub == MLSTM_HEADS + h, fb_ref[h], bias)
    pre = gt_ref[...] + bias
    cum = _log_sigmoid(pre)
    shift = 1
    while shift < CHUNK:
        cum = cum + jnp.where(pos >= shift, pltpu.roll(cum, shift, 1), 0.0)
        shift *= 2
    gate_rows = jnp.where(sub < MLSTM_HEADS, pre, cum)
    pad_rows = jnp.zeros((CHUNK - SUBLANES, CHUNK), F32)

    after_prepare()

    for ci in range(n_chunks):
        rows = pl.ds(ci * CHUNK, CHUNK)
        g_rows = gate_rows[:, ci * CHUNK:(ci + 1) * CHUNK]
        g_cols = jnp.concatenate([g_rows, pad_rows], axis=0).T
        for h in range(MLSTM_HEADS):
            qk_cols = pl.ds(h * MLSTM_QK_DIM, MLSTM_QK_DIM)
            v_cols = pl.ds(h * MLSTM_V_DIM, MLSTM_V_DIM)
            i_row = g_rows[h:h + 1, :]
            b_row = g_rows[MLSTM_HEADS + h:MLSTM_HEADS + h + 1, :]
            i_col = g_cols[:, h:h + 1]
            b_col = g_cols[:, MLSTM_HEADS + h:MLSTM_HEADS + h + 1]

            m_prev = m_ref[h, :, 0:1]
            d_log = jnp.where(causal, b_col - b_row + i_row, -jnp.inf)
            inter = b_col + m_prev
            m_t = jnp.maximum(inter, jnp.max(d_log, axis=1, keepdims=True))
            w_inter = jnp.exp(inter - m_t)
            w_intra = jnp.exp(d_log - m_t)

            q = qs_ref[rows, qk_cols]
            k = ks_ref[rows, qk_cols]
            v_aug = jnp.concatenate([v_ref[rows, v_cols], ones_col], axis=1)
            state = c_ref[h]
            s = lax.dot_general(q, k, trans_b, preferred_element_type=F32) * w_intra
            tot = jnp.dot(s.astype(BF16), v_aug, preferred_element_type=F32)
            tot = tot + w_inter * jnp.dot(q, state.astype(BF16), preferred_element_type=F32)
            num = tot[:, :MLSTM_V_DIM]
            nq = tot[:, MLSTM_V_DIM:MLSTM_V_DIM + 1]
            hid = num / jnp.maximum(jnp.abs(nq), jnp.exp(-m_t))

            b_last = b_row[:, CHUNK - 1:CHUNK]
            m_new = m_t[CHUNK - 1:CHUNK, :]
            w_state = jnp.exp(b_last + m_prev - m_new)
            w_key = jnp.exp(b_last - b_col + i_col - m_new)
            kw_t = (k.astype(F32) * w_key).T.astype(BF16)
            c_ref[h] = w_state * state + jnp.dot(kw_t, v_aug, preferred_element_type=F32)
            m_ref[h] = jnp.broadcast_to(m_new, (1, LANES))

            og = og_ref[rows, v_cols].astype(F32)
            y = hid * _rms_scale(hid) * hn_ref[:, v_cols] * jax.nn.sigmoid(og)
            o_ref[rows, v_cols] = y.astype(BF16)


N_RET_IN = 8
N_MLSTM_IN = 12


def _mixer_kernel(*refs, n_chunks):
    ret_in = refs[:N_RET_IN]
    mlstm_in = refs[N_RET_IN:N_RET_IN + N_MLSTM_IN]
    ret_o, mh_o, state_ref, c_ref, m_ref, qprev_ref, kprev_ref, qs_ref, ks_ref = refs[N_RET_IN + N_MLSTM_IN:]

    @pl.when(pl.program_id(1) == 0)
    def _():
        state_ref[...] = jnp.zeros_like(state_ref)
        c_ref[...] = jnp.zeros_like(c_ref)
        m_ref[...] = jnp.zeros_like(m_ref)
        qprev_ref[...] = jnp.zeros_like(qprev_ref)
        kprev_ref[...] = jnp.zeros_like(kprev_ref)

    _mlstm_body(*mlstm_in, mh_o, c_ref, m_ref, qprev_ref, kprev_ref, qs_ref, ks_ref, n_chunks=n_chunks,
                after_prepare=functools.partial(_retention_body, *ret_in, ret_o, state_ref, n_chunks=n_chunks))


def _mixer_heads(proj, gates_t, log_gamma, cos2, sin2, ret_head_norm, conv_w, conv_b, igate_b, fgate_b,
                 mlstm_head_norm, batch, seq):
    m = proj.shape[0]
    tr = _pick_tile(seq, 512, CHUNK)
    tiles = seq // tr
    base = 2 * RET_QK_W + 2 * RET_V_W
    assert RET_V_W == 2 * RET_QK_W and base % MLSTM_V_W == 0 and MLSTM_V_W == 2 * MLSTM_QK_W
    q_blk = base // MLSTM_QK_W
    v_blk = (base + 2 * MLSTM_QK_W) // MLSTM_V_W
    smem = pl.BlockSpec(memory_space=pltpu.SMEM)
    return pl.pallas_call(
        functools.partial(_mixer_kernel, n_chunks=tr // CHUNK),
        out_shape=(jax.ShapeDtypeStruct((m, RET_V_W), BF16), jax.ShapeDtypeStruct((m, MLSTM_V_W), BF16)),
        grid=(batch, tiles),
        in_specs=[
            smem,
            pl.BlockSpec((tr, RET_QK_W), lambda b, c: (b * tiles + c, 0)),
            pl.BlockSpec((tr, RET_QK_W), lambda b, c: (b * tiles + c, 1)),
            pl.BlockSpec((tr, RET_V_W), lambda b, c: (b * tiles + c, 1)),
            pl.BlockSpec((tr, RET_V_W), lambda b, c: (b * tiles + c, 2)),
            pl.BlockSpec((tr, RET_QK_DIM), lambda b, c: (c, 0)),
            pl.BlockSpec((tr, RET_QK_DIM), lambda b, c: (c, 0)),
            pl.BlockSpec((1, RET_V_W), lambda b, c: (0, 0)),
            smem,
            smem,
            pl.BlockSpec((tr, MLSTM_QK_W), lambda b, c: (b * tiles + c, q_blk)),
            pl.BlockSpec((tr, MLSTM_QK_W), lambda b, c: (b * tiles + c, q_blk + 1)),
            pl.BlockSpec((tr, MLSTM_V_W), lambda b, c: (b * tiles + c, v_blk)),
            pl.BlockSpec((tr, MLSTM_V_W), lambda b, c: (b * tiles + c, v_blk + 1)),
            pl.BlockSpec((SUBLANES, tr), lambda b, c: (0, b * tiles + c)),
            pl.BlockSpec((CONV_WIDTH, MLSTM_QK_W), lambda b, c: (0, 0)),
            pl.BlockSpec((CONV_WIDTH, MLSTM_QK_W), lambda b, c: (0, 1)),
            pl.BlockSpec((1, MLSTM_QK_W), lambda b, c: (0, 0)),
            pl.BlockSpec((1, MLSTM_QK_W), lambda b, c: (0, 1)),
            pl.BlockSpec((1, MLSTM_V_W), lambda b, c: (0, 0)),
        ],
        out_specs=(
            pl.BlockSpec((tr, RET_V_W), lambda b, c: (b * tiles + c, 0)),
            pl.BlockSpec((tr, MLSTM_V_W), lambda b, c: (b * tiles + c, 0)),
        ),
        scratch_shapes=[
            pltpu.VMEM((RET_HEADS, RET_QK_DIM, RET_V_DIM), F32),
            pltpu.VMEM((MLSTM_HEADS, MLSTM_QK_DIM, MLSTM_V_DIM + AUG_W), F32),
            pltpu.VMEM((MLSTM_HEADS, 1, LANES), F32),
            pltpu.VMEM((CHUNK, MLSTM_QK_W), BF16),
            pltpu.VMEM((CHUNK, MLSTM_QK_W), BF16),
            pltpu.VMEM((tr, MLSTM_QK_W), BF16),
            pltpu.VMEM((tr, MLSTM_QK_W), BF16),
        ],
        compiler_params=pltpu.CompilerParams(
            dimension_semantics=("parallel", "arbitrary"), vmem_limit_bytes=V7X_VMEM_LIMIT_BYTES
        ),
        name="mixer_heads",
    )(log_gamma, proj, proj, proj, proj, cos2, sin2, ret_head_norm.reshape(1, RET_V_W),
      igate_b, fgate_b, proj, proj, proj, proj, gates_t, conv_w, conv_w, conv_b.reshape(1, -1),
      conv_b.reshape(1, -1), mlstm_head_norm.reshape(1, MLSTM_V_W))


def _outproj_kernel(x_ref, r_ref, h_ref, wr_ref, wh_ref, o_ref):
    acc = jnp.dot(r_ref[...], wr_ref[...], preferred_element_type=F32)
    acc = acc + jnp.dot(h_ref[...], wh_ref[...], preferred_element_type=F32)
    o_ref[...] = x_ref[...] + acc


def _out_proj(x2d, ret, mh, wo):
    m, d = x2d.shape
    tm = _pick_tile(m, 1024, SUBLANES)
    tn = _pick_tile(d, 1024, LANES)
    return pl.pallas_call(
        _outproj_kernel,
        out_shape=jax.ShapeDtypeStruct((m, d), F32),
        grid=(m // tm, d // tn),
        in_specs=[
            pl.BlockSpec((tm, tn), lambda i, n: (i, n)),
            pl.BlockSpec((tm, RET_V_W), lambda i, n: (i, 0)),
            pl.BlockSpec((tm, MLSTM_V_W), lambda i, n: (i, 0)),
            pl.BlockSpec((RET_V_W, tn), lambda i, n: (0, n)),
            pl.BlockSpec((MLSTM_V_W, tn), lambda i, n: (1, n)),
        ],
        out_specs=pl.BlockSpec((tm, tn), lambda i, n: (i, n)),
        compiler_params=pltpu.CompilerParams(
            dimension_semantics=("parallel", "parallel"), vmem_limit_bytes=V7X_VMEM_LIMIT_BYTES
        ),
        name="out_proj",
    )(x2d, ret, mh, wo, wo)


def _rotary_tables(seq):
    half = RET_QK_DIM // 2
    inv_freq = ROPE_BASE ** (-jnp.arange(half, dtype=F32) / half)
    ang = jnp.arange(seq, dtype=F32)[:, None] * inv_freq[None, :]
    cos, sin = jnp.cos(ang), jnp.sin(ang)
    return jnp.concatenate([cos, cos], axis=1), jnp.concatenate([-sin, sin], axis=1)


def kernel(x, ffn1_norm, ffn1_w_gate, ffn1_w_up, ffn1_w_down, mix_norm, w_in, conv_w, conv_b, igate_b, fgate_b,
           ret_head_norm, mlstm_head_norm, w_out, ffn2_norm, ffn2_w_gate, ffn2_w_up, ffn2_w_down, final_norm):
    batch, seq, d = x.shape
    depth = ffn1_norm.shape[0]
    assert seq % CHUNK == 0 and w_in.shape[-1] == PROJ_W + N_GATES and w_out.shape[1] == RET_V_W + MLSTM_V_W
    log_gamma = jnp.log1p(-jnp.exp2(-5.0 - jnp.arange(RET_HEADS, dtype=F32)))
    cos2, sin2 = _rotary_tables(seq)
    h = x.reshape(batch * seq, d)
    for layer in range(depth):
        last = layer == depth - 1
        h, xn = _ffn(h, ffn1_norm[layer], ffn1_w_gate[layer], ffn1_w_up[layer], ffn1_w_down[layer], mix_norm[layer],
                     epilogue="emit")
        proj, gates_t, wo = _in_proj(xn, w_in[layer], w_out[layer])
        ret, mh = _mixer_heads(proj, gates_t, log_gamma, cos2, sin2, ret_head_norm[layer], conv_w[layer],
                               conv_b[layer], igate_b[layer], fgate_b[layer], mlstm_head_norm[layer], batch, seq)
        h = _out_proj(h, ret, mh, wo)
        h = _ffn(h, ffn2_norm[layer], ffn2_w_gate[layer], ffn2_w_up[layer], ffn2_w_down[layer], final_norm,
                 epilogue="final" if last else "none")
    return h.reshape(batch, seq, d)
```

```python
import functools

import jax
import jax.numpy as jnp
from jax import lax
from jax.experimental import pallas as pl
from jax.experimental.pallas import tpu as pltpu

F32 = jnp.float32
BF16 = jnp.bfloat16

RET_HEADS = 8
RET_QK_DIM = 128
RET_V_DIM = 256
MLSTM_HEADS = 4
MLSTM_QK_DIM = 256
MLSTM_V_DIM = 512
CONV_WIDTH = 4
CHUNK = 128
ROPE_BASE = 10000.0
NORM_EPS = 1e-6
FFN_RES_WEIGHT = 0.5

RET_QK_W = RET_HEADS * RET_QK_DIM
RET_V_W = RET_HEADS * RET_V_DIM
MLSTM_QK_W = MLSTM_HEADS * MLSTM_QK_DIM
MLSTM_V_W = MLSTM_HEADS * MLSTM_V_DIM
PROJ_W = 2 * RET_QK_W + 2 * RET_V_W + 2 * MLSTM_QK_W + 2 * MLSTM_V_W
N_GATES = 2 * MLSTM_HEADS

LANES = 128
SUBLANES = 8
V7X_VMEM_LIMIT_BYTES = 58 * 1024 * 1024

AUG_W = LANES


def _pick_tile(total, preferred, quantum):
    if total <= preferred:
        return total
    t = (preferred // quantum) * quantum
    while t >= quantum:
        if total % t == 0:
            return t
        t -= quantum
    return total


def _rms_scale(x):
    return lax.rsqrt(jnp.mean(x * x, axis=-1, keepdims=True) + NORM_EPS)


FFN_NORM_ROWS = 128
FFN_SLAB_DMA_PRIORITY = 1


def _ffn_kernel(x_hbm, g_ref, wg_ref, wu_ref, wd_ref, eg_ref, *rest, tm, tf, epilogue):
    if epilogue == "emit":
        o_hbm, n_hbm, acc_ref, xn_ref, sem = rest
    else:
        o_hbm, acc_ref, xn_ref, sem = rest
    i = pl.program_id(0)
    f = pl.program_id(1)
    n_tiles = pl.num_programs(0)
    last_f = pl.num_programs(1) - 1

    n_slabs = tm // FFN_NORM_ROWS

    def slab(r):
        return pl.ds(pl.multiple_of(r * FFN_NORM_ROWS, FFN_NORM_ROWS), FFN_NORM_ROWS)

    def hbm_rows(tile, r):
        return pl.ds(pl.multiple_of(tile * tm + r * FFN_NORM_ROWS, FFN_NORM_ROWS), FFN_NORM_ROWS)

    def x_copy(tile, r):
        return pltpu.make_async_copy(x_hbm.at[hbm_rows(tile, r)], acc_ref.at[slab(r)], sem.at[0, r])

    def y_copy(tile, r):
        return pltpu.make_async_copy(acc_ref.at[slab(r)], o_hbm.at[hbm_rows(tile, r)], sem.at[1, r])

    def n_copy(tile, r):
        return pltpu.make_async_copy(xn_ref.at[slab(r)], n_hbm.at[hbm_rows(tile, r)], sem.at[2, r])

    @pl.when(f == 0)
    def _():
        def load_rows(r, carry):
            @pl.when(i > 0)
            def _():
                y_copy(i - 1, r).wait()

            x_copy(i, r).start(priority=FFN_SLAB_DMA_PRIORITY)
            return carry

        lax.fori_loop(0, n_slabs, load_rows, 0)

        def norm_rows(r, carry):
            x_copy(i, r).wait()
            if epilogue == "emit":
                @pl.when(i > 0)
                def _():
                    n_copy(i - 1, r).wait()

            x = acc_ref[slab(r), :]
            xn_ref[slab(r), :] = (x * _rms_scale(x) * g_ref[...]).astype(BF16)
            return carry

        lax.fori_loop(0, n_slabs, norm_rows, 0)

    wgu = jnp.concatenate([wg_ref[...].astype(BF16), wu_ref[...].astype(BF16)], axis=1)
    gu = jnp.dot(xn_ref[...], wgu, preferred_element_type=F32)
    g = gu[:, :tf]
    u = gu[:, tf:]
    h = (g * jax.nn.sigmoid(g) * u * FFN_RES_WEIGHT).astype(BF16)
    acc_ref[...] += jnp.dot(h, wd_ref[...].astype(BF16), preferred_element_type=F32)

    @pl.when(f == last_f)
    def _():
        def finish_rows(r, carry):
            if epilogue == "final":
                y = acc_ref[slab(r), :]
                acc_ref[slab(r), :] = y * _rms_scale(y) * eg_ref[...]
            elif epilogue == "emit":
                y = acc_ref[slab(r), :]
                xn_ref[slab(r), :] = (y * _rms_scale(y) * eg_ref[...]).astype(BF16)
                n_copy(i, r).start()
            y_copy(i, r).start()
            return carry

        lax.fori_loop(0, n_slabs, finish_rows, 0)

        @pl.when(i == n_tiles - 1)
        def _():
            def drain_rows(r, carry):
                y_copy(i, r).wait()
                if epilogue == "emit":
                    n_copy(i, r).wait()
                return carry

            lax.fori_loop(0, n_slabs, drain_rows, 0)


def _ffn(x2d, norm_g, w_gate, w_up, w_down, epilogue_g, *, epilogue):
    m, d = x2d.shape
    dff = w_gate.shape[1]
    tm = _pick_tile(m, 1024, FFN_NORM_ROWS)
    tf = _pick_tile(dff, 256, LANES)
    assert tm % FFN_NORM_ROWS == 0 and epilogue in ("none", "final", "emit")
    hbm = pl.BlockSpec(memory_space=pl.ANY)
    y_shape = jax.ShapeDtypeStruct((m, d), F32)
    emit = epilogue == "emit"
    return pl.pallas_call(
        functools.partial(_ffn_kernel, tm=tm, tf=tf, epilogue=epilogue),
        out_shape=(y_shape, jax.ShapeDtypeStruct((m, d), BF16)) if emit else y_shape,
        grid=(m // tm, dff // tf),
        in_specs=[
            hbm,
            pl.BlockSpec((1, d), lambda i, f: (0, 0)),
            pl.BlockSpec((d, tf), lambda i, f: (0, f)),
            pl.BlockSpec((d, tf), lambda i, f: (0, f)),
            pl.BlockSpec((tf, d), lambda i, f: (f, 0)),
            pl.BlockSpec((1, d), lambda i, f: (0, 0)),
        ],
        out_specs=(hbm, hbm) if emit else hbm,
        scratch_shapes=[
            pltpu.VMEM((tm, d), F32),
            pltpu.VMEM((tm, d), BF16),
            pltpu.SemaphoreType.DMA((3, tm // FFN_NORM_ROWS)),
        ],
        compiler_params=pltpu.CompilerParams(
            dimension_semantics=("arbitrary", "arbitrary"), vmem_limit_bytes=V7X_VMEM_LIMIT_BYTES
        ),
        name="ffn_" + epilogue,
    )(x2d, norm_g.reshape(1, d), w_gate, w_up, w_down, epilogue_g.reshape(1, d))


INPROJ_SUB = 2 * LANES


def _inproj_kernel(xn_ref, wt_ref, wgt_ref, wo_ref, p_ref, gt_ref, wo_bf_ref):
    n = pl.program_id(1)
    trans_b = (((1,), (1,)), ((), ()))

    wo_bf_ref[...] = wo_ref[...].astype(BF16)

    @pl.when(n == 0)
    def _():
        wg = jnp.concatenate([wgt_ref[...], jnp.zeros_like(wgt_ref)], axis=0).astype(BF16)
        gt = lax.dot_general(wg, xn_ref[...], trans_b, preferred_element_type=F32)
        gt_ref[...] = gt[:SUBLANES, :]

    for j in range(wt_ref.shape[0] // INPROJ_SUB):
        cols = pl.ds(j * INPROJ_SUB, INPROJ_SUB)
        w = wt_ref[cols, :].astype(BF16)
        p_ref[:, cols] = lax.dot_general(xn_ref[...], w, trans_b, preferred_element_type=F32).astype(BF16)


def _in_proj(xn, w_in, w_out):
    m, d = xn.shape
    assert N_GATES == SUBLANES and PROJ_W % SUBLANES == 0
    tm = _pick_tile(m, 1024, LANES)
    tn = _pick_tile(PROJ_W, 1024, INPROJ_SUB)
    n_cols = PROJ_W // tn
    n_steps = (m // tm) * n_cols
    wo_rows, wo_cols = w_out.shape
    cast_rows = next(r for r in range(2 * SUBLANES, wo_rows + 1, 2 * SUBLANES)
                     if wo_rows % r == 0 and wo_rows // r <= n_steps)
    n_cast = wo_rows // cast_rows

    def cast_block(i, n):
        return (jnp.minimum(i * n_cols + n, n_cast - 1), 0)

    w_t = w_in.T
    return pl.pallas_call(
        _inproj_kernel,
        out_shape=(
            jax.ShapeDtypeStruct((m, PROJ_W), BF16),
            jax.ShapeDtypeStruct((SUBLANES, m), F32),
            jax.ShapeDtypeStruct((wo_rows, wo_cols), BF16),
        ),
        grid=(m // tm, n_cols),
        in_specs=[
            pl.BlockSpec((tm, d), lambda i, n: (i, 0)),
            pl.BlockSpec((tn, d), lambda i, n: (n, 0)),
            pl.BlockSpec((N_GATES, d), lambda i, n: (PROJ_W // N_GATES, 0)),
            pl.BlockSpec((cast_rows, wo_cols), cast_block),
        ],
        out_specs=(
            pl.BlockSpec((tm, tn), lambda i, n: (i, n)),
            pl.BlockSpec((SUBLANES, tm), lambda i, n: (0, i)),
            pl.BlockSpec((cast_rows, wo_cols), cast_block),
        ),
        compiler_params=pltpu.CompilerParams(
            dimension_semantics=("arbitrary", "arbitrary"), vmem_limit_bytes=V7X_VMEM_LIMIT_BYTES
        ),
        name="in_proj",
    )(xn, w_t, w_t, w_out)


def _retention_body(lg_ref, q_ref, k_ref, v_ref, g_ref, cos_ref, sin_ref, hn_ref, o_ref, state_ref, *, n_chunks):
    row = lax.broadcasted_iota(jnp.int32, (CHUNK, CHUNK), 0)
    col = lax.broadcasted_iota(jnp.int32, (CHUNK, CHUNK), 1)
    rel = (row - col).astype(F32)
    pos = lax.broadcasted_iota(jnp.int32, (CHUNK, 1), 0).astype(F32)
    k_scale = RET_QK_DIM ** -0.5
    trans_b = (((1,), (1,)), ((), ()))

    decays = []
    for h in range(RET_HEADS):
        lg = lg_ref[h]
        decays.append((
            jnp.where(rel >= 0, jnp.exp(lg * rel), 0.0),
            jnp.exp(lg * (pos + 1.0)),
            jnp.exp(lg * (CHUNK - 1.0 - pos)),
            jnp.exp(lg * CHUNK),
        ))

    for ci in range(n_chunks):
        rows = pl.ds(ci * CHUNK, CHUNK)
        cos = cos_ref[rows, :]
        sin = sin_ref[rows, :]
        for h in range(RET_HEADS):
            intra, q_decay, k_decay, chunk_decay = decays[h]
            qk_cols = pl.ds(h * RET_QK_DIM, RET_QK_DIM)
            v_cols = pl.ds(h * RET_V_DIM, RET_V_DIM)
            q = q_ref[rows, qk_cols].astype(F32)
            k = k_ref[rows, qk_cols].astype(F32)
            q = q * cos + pltpu.roll(q, RET_QK_DIM // 2, 1) * sin
            k = (k * cos + pltpu.roll(k, RET_QK_DIM // 2, 1) * sin) * k_scale
            qb = q.astype(BF16)
            v = v_ref[rows, v_cols]
            state = state_ref[h]
            scores = lax.dot_general(qb, k.astype(BF16), trans_b, preferred_element_type=F32) * intra
            out = jnp.dot(scores.astype(BF16), v, preferred_element_type=F32)
            out = out + jnp.dot(qb, state.astype(BF16), preferred_element_type=F32) * q_decay
            kd_t = (k * k_decay).T.astype(BF16)
            state_ref[h] = state * chunk_decay + jnp.dot(kd_t, v, preferred_element_type=F32)
            gate = g_ref[rows, v_cols].astype(F32)
            y = out * _rms_scale(out) * hn_ref[:, v_cols] * (gate * jax.nn.sigmoid(gate))
            o_ref[rows, v_cols] = y.astype(BF16)


def _log_sigmoid(x):
    return jnp.minimum(x, 0.0) - jnp.log1p(jnp.exp(-jnp.abs(x)))


def _mlstm_body(ib_ref, fb_ref, q_ref, k_ref, v_ref, og_ref, gt_ref, cwq_ref, cwk_ref, cbq_ref, cbk_ref, hn_ref,
                o_ref, c_ref, m_ref, qprev_ref, kprev_ref, qs_ref, ks_ref, *, n_chunks, after_prepare):
    tr = n_chunks * CHUNK

    sel_row = lax.broadcasted_iota(jnp.int32, (CHUNK, 2 * CHUNK), 0)
    sel_col = lax.broadcasted_iota(jnp.int32, (CHUNK, 2 * CHUNK), 1)
    selectors = [(sel_col == sel_row + CHUNK - (CONV_WIDTH - 1 - j)).astype(BF16) for j in range(CONV_WIDTH - 1)]

    def conv_silu(prev_ref, src_ref, w_ref, b_ref, dst_ref, scale):
        prev = prev_ref[...]
        for ci in range(n_chunks):
            cur = src_ref[pl.ds(ci * CHUNK, CHUNK), :]
            both = jnp.concatenate([prev, cur], axis=0)
            acc = b_ref[...] + cur.astype(F32) * w_ref[pl.ds(CONV_WIDTH - 1, 1), :]
            for j in range(CONV_WIDTH - 1):
                shifted = jnp.dot(selectors[j], both, preferred_element_type=F32)
                acc = acc + shifted * w_ref[pl.ds(j, 1), :]
            dst_ref[pl.ds(ci * CHUNK, CHUNK), :] = (acc * jax.nn.sigmoid(acc) * scale).astype(BF16)
            prev = cur
        prev_ref[...] = prev

    conv_silu(qprev_ref, q_ref, cwq_ref, cbq_ref, qs_ref, 1.0)
    conv_silu(kprev_ref, k_ref, cwk_ref, cbk_ref, ks_ref, MLSTM_QK_DIM ** -0.5)

    row = lax.broadcasted_iota(jnp.int32, (CHUNK, CHUNK), 0)
    col = lax.broadcasted_iota(jnp.int32, (CHUNK, CHUNK), 1)
    causal = col <= row
    ones_col = (lax.broadcasted_iota(jnp.int32, (CHUNK, AUG_W), 1) == 0).astype(BF16)
    trans_b = (((1,), (1,)), ((), ()))

    sub = lax.broadcasted_iota(jnp.int32, (SUBLANES, tr), 0)
    pos = lax.broadcasted_iota(jnp.int32, (SUBLANES, tr), 1) % CHUNK
    bias = jnp.zeros((SUBLANES, tr), F32)
    for h in range(MLSTM_HEADS):
        bias = jnp.where(sub == h, ib_ref[h], bias)
        bias = jnp.where(sub == MLSTM_HEADS + h, fb_ref[h], bias)
    pre = gt_ref[...] + bias
    cum = _log_sigmoid(pre)
    shift = 1
    while shift < CHUNK:
        cum = cum + jnp.where(pos >= shift, pltpu.roll(cum, shift, 1), 0.0)
        shift *= 2
    gate_rows = jnp.where(sub < MLSTM_HEADS, pre, cum)
    pad_rows = jnp.zeros((CHUNK - SUBLANES, CHUNK), F32)

    after_prepare()

    for ci in range(n_chunks):
        rows = pl.ds(ci * CHUNK, CHUNK)
        g_rows = gate_rows[:, ci * CHUNK:(ci + 1) * CHUNK]
        g_cols = jnp.concatenate([g_rows, pad_rows], axis=0).T
        for h in range(MLSTM_HEADS):
            qk_cols = pl.ds(h * MLSTM_QK_DIM, MLSTM_QK_DIM)
            v_cols = pl.ds(h * MLSTM_V_DIM, MLSTM_V_DIM)
            i_row = g_rows[h:h + 1, :]
            b_row = g_rows[MLSTM_HEADS + h:MLSTM_HEADS + h + 1, :]
            i_col = g_cols[:, h:h + 1]
            b_col = g_cols[:, MLSTM_HEADS + h:MLSTM_HEADS + h + 1]

            m_prev = m_ref[h, :, 0:1]
            d_log = jnp.where(causal, b_col - b_row + i_row, -jnp.inf)
            inter = b_col + m_prev
            m_t = jnp.maximum(inter, jnp.max(d_log, axis=1, keepdims=True))
            w_inter = jnp.exp(inter - m_t)
            w_intra = jnp.exp(d_log - m_t)

            q = qs_ref[rows, qk_cols]
            k = ks_ref[rows, qk_cols]
            v_aug = jnp.concatenate([v_ref[rows, v_cols], ones_col], axis=1)
            state = c_ref[h]
            s = lax.dot_general(q, k, trans_b, preferred_element_type=F32) * w_intra
            tot = jnp.dot(s.astype(BF16), v_aug, preferred_element_type=F32)
            tot = tot + w_inter * jnp.dot(q, state.astype(BF16), preferred_element_type=F32)
            num = tot[:, :MLSTM_V_DIM]
            nq = tot[:, MLSTM_V_DIM:MLSTM_V_DIM + 1]
            hid = num / jnp.maximum(jnp.abs(nq), jnp.exp(-m_t))

            b_last = b_row[:, CHUNK - 1:CHUNK]
            m_new = m_t[CHUNK - 1:CHUNK, :]
            w_state = jnp.exp(b_last + m_prev - m_new)
            w_key = jnp.exp(b_last - b_col + i_col - m_new)
            kw_t = (k.astype(F32) * w_key).T.astype(BF16)
            c_ref[h] = w_state * state + jnp.dot(kw_t, v_aug, preferred_element_type=F32)
            m_ref[h] = jnp.broadcast_to(m_new, (1, LANES))

            og = og_ref[rows, v_cols].astype(F32)
            y = hid * _rms_scale(hid) * hn_ref[:, v_cols] * jax.nn.sigmoid(og)
            o_ref[rows, v_cols] = y.astype(BF16)


N_RET_IN = 8
N_MLSTM_IN = 12


def _mixer_kernel(*refs, n_chunks):
    ret_in = refs[:N_RET_IN]
    mlstm_in = refs[N_RET_IN:N_RET_IN + N_MLSTM_IN]
    ret_o, mh_o, state_ref, c_ref, m_ref, qprev_ref, kprev_ref, qs_ref, ks_ref = refs[N_RET_IN + N_MLSTM_IN:]

    @pl.when(pl.program_id(1) == 0)
    def _():
        state_ref[...] = jnp.zeros_like(state_ref)
        c_ref[...] = jnp.zeros_like(c_ref)
        m_ref[...] = jnp.zeros_like(m_ref)
        qprev_ref[...] = jnp.zeros_like(qprev_ref)
        kprev_ref[...] = jnp.zeros_like(kprev_ref)

    _mlstm_body(*mlstm_in, mh_o, c_ref, m_ref, qprev_ref, kprev_ref, qs_ref, ks_ref, n_chunks=n_chunks,
                after_prepare=functools.partial(_retention_body, *ret_in, ret_o, state_ref, n_chunks=n_chunks))


def _mixer_heads(proj, gates_t, log_gamma, cos2, sin2, ret_head_norm, conv_w, conv_b, igate_b, fgate_b,
                 mlstm_head_norm, batch, seq):
    m = proj.shape[0]
    tr = _pick_tile(seq, 512, CHUNK)
    tiles = seq // tr
    base = 2 * RET_QK_W + 2 * RET_V_W
    assert RET_V_W == 2 * RET_QK_W and base % MLSTM_V_W == 0 and MLSTM_V_W == 2 * MLSTM_QK_W
    q_blk = base // MLSTM_QK_W
    v_blk = (base + 2 * MLSTM_QK_W) // MLSTM_V_W
    smem = pl.BlockSpec(memory_space=pltpu.SMEM)
    return pl.pallas_call(
        functools.partial(_mixer_kernel, n_chunks=tr // CHUNK),
        out_shape=(jax.ShapeDtypeStruct((m, RET_V_W), BF16), jax.ShapeDtypeStruct((m, MLSTM_V_W), BF16)),
        grid=(batch, tiles),
        in_specs=[
            smem,
            pl.BlockSpec((tr, RET_QK_W), lambda b, c: (b * tiles + c, 0)),
            pl.BlockSpec((tr, RET_QK_W), lambda b, c: (b * tiles + c, 1)),
            pl.BlockSpec((tr, RET_V_W), lambda b, c: (b * tiles + c, 1)),
            pl.BlockSpec((tr, RET_V_W), lambda b, c: (b * tiles + c, 2)),
            pl.BlockSpec((tr, RET_QK_DIM), lambda b, c: (c, 0)),
            pl.BlockSpec((tr, RET_QK_DIM), lambda b, c: (c, 0)),
            pl.BlockSpec((1, RET_V_W), lambda b, c: (0, 0)),
            smem,
            smem,
            pl.BlockSpec((tr, MLSTM_QK_W), lambda b, c: (b * tiles + c, q_blk)),
            pl.BlockSpec((tr, MLSTM_QK_W), lambda b, c: (b * tiles + c, q_blk + 1)),
            pl.BlockSpec((tr, MLSTM_V_W), lambda b, c: (b * tiles + c, v_blk)),
            pl.BlockSpec((tr, MLSTM_V_W), lambda b, c: (b * tiles + c, v_blk + 1)),
            pl.BlockSpec((SUBLANES, tr), lambda b, c: (0, b * tiles + c)),
            pl.BlockSpec((CONV_WIDTH, MLSTM_QK_W), lambda b, c: (0, 0)),
            pl.BlockSpec((CONV_WIDTH, MLSTM_QK_W), lambda b, c: (0, 1)),
            pl.BlockSpec((1, MLSTM_QK_W), lambda b, c: (0, 0)),
            pl.BlockSpec((1, MLSTM_QK_W), lambda b, c: (0, 1)),
            pl.BlockSpec((1, MLSTM_V_W), lambda b, c: (0, 0)),
        ],
        out_specs=(
            pl.BlockSpec((tr, RET_V_W), lambda b, c: (b * tiles + c, 0)),
            pl.BlockSpec((tr, MLSTM_V_W), lambda b, c: (b * tiles + c, 0)),
        ),
        scratch_shapes=[
            pltpu.VMEM((RET_HEADS, RET_QK_DIM, RET_V_DIM), F32),
            pltpu.VMEM((MLSTM_HEADS, MLSTM_QK_DIM, MLSTM_V_DIM + AUG_W), F32),
            pltpu.VMEM((MLSTM_HEADS, 1, LANES), F32),
            pltpu.VMEM((CHUNK, MLSTM_QK_W), BF16),
            pltpu.VMEM((CHUNK, MLSTM_QK_W), BF16),
            pltpu.VMEM((tr, MLSTM_QK_W), BF16),
            pltpu.VMEM((tr, MLSTM_QK_W), BF16),
        ],
        compiler_params=pltpu.CompilerParams(
            dimension_semantics=("parallel", "arbitrary"), vmem_limit_bytes=V7X_VMEM_LIMIT_BYTES
        ),
        name="mixer_heads",
    )(log_gamma, proj, proj, proj, proj, cos2, sin2, ret_head_norm.reshape(1, RET_V_W),
      igate_b, fgate_b, proj, proj, proj, proj, gates_t, conv_w, conv_w, conv_b.reshape(1, -1),
      conv_b.reshape(1, -1), mlstm_head_norm.reshape(1, MLSTM_V_W))


def _outproj_kernel(x_ref, r_ref, h_ref, wr_ref, wh_ref, o_ref):
    acc = jnp.dot(r_ref[...], wr_ref[...], preferred_element_type=F32)
    acc = acc + jnp.dot(h_ref[...], wh_ref[...], preferred_element_type=F32)
    o_ref[...] = x_ref[...] + acc


def _out_proj(x2d, ret, mh, wo):
    m, d = x2d.shape
    tm = _pick_tile(m, 1024, SUBLANES)
    tn = _pick_tile(d, 1024, LANES)
    return pl.pallas_call(
        _outproj_kernel,
        out_shape=jax.ShapeDtypeStruct((m, d), F32),
        grid=(m // tm, d // tn),
        in_specs=[
            pl.BlockSpec((tm, tn), lambda i, n: (i, n)),
            pl.BlockSpec((tm, RET_V_W), lambda i, n: (i, 0)),
            pl.BlockSpec((tm, MLSTM_V_W), lambda i, n: (i, 0)),
            pl.BlockSpec((RET_V_W, tn), lambda i, n: (0, n)),
            pl.BlockSpec((MLSTM_V_W, tn), lambda i, n: (1, n)),
        ],
        out_specs=pl.BlockSpec((tm, tn), lambda i, n: (i, n)),
        compiler_params=pltpu.CompilerParams(
            dimension_semantics=("parallel", "parallel"), vmem_limit_bytes=V7X_VMEM_LIMIT_BYTES
        ),
        name="out_proj",
    )(x2d, ret, mh, wo, wo)


def _rotary_tables(seq):
    half = RET_QK_DIM // 2
    inv_freq = ROPE_BASE ** (-jnp.arange(half, dtype=F32) / half)
    ang = jnp.arange(seq, dtype=F32)[:, None] * inv_freq[None, :]
    cos, sin = jnp.cos(ang), jnp.sin(ang)
    return jnp.concatenate([cos, cos], axis=1), jnp.concatenate([-sin, sin], axis=1)


def kernel(x, ffn1_norm, ffn1_w_gate, ffn1_w_up, ffn1_w_down, mix_norm, w_in, conv_w, conv_b, igate_b, fgate_b,
           ret_head_norm, mlstm_head_norm, w_out, ffn2_norm, ffn2_w_gate, ffn2_w_up, ffn2_w_down, final_norm):
    batch, seq, d = x.shape
    depth = ffn1_norm.shape[0]
    assert seq % CHUNK == 0 and w_in.shape[-1] == PROJ_W + N_GATES and w_out.shape[1] == RET_V_W + MLSTM_V_W
    log_gamma = jnp.log1p(-jnp.exp2(-5.0 - jnp.arange(RET_HEADS, dtype=F32)))
    cos2, sin2 = _rotary_tables(seq)
    h = x.reshape(batch * seq, d)
    for layer in range(depth):
        last = layer == depth - 1
        h, xn = _ffn(h, ffn1_norm[layer], ffn1_w_gate[layer], ffn1_w_up[layer], ffn1_w_down[layer], mix_norm[layer],
                     epilogue="emit")
        proj, gates_t, wo = _in_proj(xn, w_in[layer], w_out[layer])
        ret, mh = _mixer_heads(proj, gates_t, log_gamma, cos2, sin2, ret_head_norm[layer], conv_w[layer],
                               conv_b[layer], igate_b[layer], fgate_b[layer], mlstm_head_norm[layer], batch, seq)
        h = _out_proj(h, ret, mh, wo)
        h = _ffn(h, ffn2_norm[layer], ffn2_w_gate[layer], ffn2_w_up[layer], ffn2_w_down[layer], final_norm,
                 epilogue="final" if last else "none")
    return h.reshape(batch, seq, d)
```

```python
import functools

import jax
import jax.numpy as jnp
from jax import lax
from jax.experimental import pallas as pl
from jax.experimental.pallas import tpu as pltpu

F32 = jnp.float32
BF16 = jnp.bfloat16

RET_HEADS = 8
RET_QK_DIM = 128
RET_V_DIM = 256
MLSTM_HEADS = 4
MLSTM_QK_DIM = 256
MLSTM_V_DIM = 512
CONV_WIDTH = 4
CHUNK = 128
ROPE_BASE = 10000.0
NORM_EPS = 1e-6
FFN_RES_WEIGHT = 0.5

RET_QK_W = RET_HEADS * RET_QK_DIM
RET_V_W = RET_HEADS * RET_V_DIM
MLSTM_QK_W = MLSTM_HEADS * MLSTM_QK_DIM
MLSTM_V_W = MLSTM_HEADS * MLSTM_V_DIM
PROJ_W = 2 * RET_QK_W + 2 * RET_V_W + 2 * MLSTM_QK_W + 2 * MLSTM_V_W
N_GATES = 2 * MLSTM_HEADS

LANES = 128
SUBLANES = 8
V7X_VMEM_LIMIT_BYTES = 58 * 1024 * 1024

AUG_W = LANES


def _pick_tile(total, preferred, quantum):
    if total <= preferred:
        return total
    t = (preferred // quantum) * quantum
    while t >= quantum:
        if total % t == 0:
            return t
        t -= quantum
    return total


def _rms_scale(x):
    return lax.rsqrt(jnp.mean(x * x, axis=-1, keepdims=True) + NORM_EPS)


FFN_NORM_ROWS = 128


def _ffn_kernel(x_hbm, g_ref, wg_ref, wu_ref, wd_ref, eg_ref, *rest, tm, tf, epilogue):
    if epilogue == "emit":
        o_hbm, n_hbm, acc_ref, xn_ref, sem = rest
    else:
        o_hbm, acc_ref, xn_ref, sem = rest
    i = pl.program_id(0)
    f = pl.program_id(1)
    n_tiles = pl.num_programs(0)
    last_f = pl.num_programs(1) - 1

    n_slabs = tm // FFN_NORM_ROWS

    def slab(r):
        return pl.ds(pl.multiple_of(r * FFN_NORM_ROWS, FFN_NORM_ROWS), FFN_NORM_ROWS)

    def hbm_rows(tile, r):
        return pl.ds(pl.multiple_of(tile * tm + r * FFN_NORM_ROWS, FFN_NORM_ROWS), FFN_NORM_ROWS)

    def x_copy(tile, r):
        return pltpu.make_async_copy(x_hbm.at[hbm_rows(tile, r)], acc_ref.at[slab(r)], sem.at[0, r])

    def y_copy(tile, r):
        return pltpu.make_async_copy(acc_ref.at[slab(r)], o_hbm.at[hbm_rows(tile, r)], sem.at[1, r])

    def n_copy(tile, r):
        return pltpu.make_async_copy(xn_ref.at[slab(r)], n_hbm.at[hbm_rows(tile, r)], sem.at[2, r])

    @pl.when(f == 0)
    def _():
        def load_rows(r, carry):
            @pl.when(i > 0)
            def _():
                y_copy(i - 1, r).wait()

            x_copy(i, r).start()
            return carry

        lax.fori_loop(0, n_slabs, load_rows, 0)

        def norm_rows(r, carry):
            x_copy(i, r).wait()
            if epilogue == "emit":
                @pl.when(i > 0)
                def _():
                    n_copy(i - 1, r).wait()

            x = acc_ref[slab(r), :]
            xn_ref[slab(r), :] = (x * _rms_scale(x) * g_ref[...]).astype(BF16)
            return carry

        lax.fori_loop(0, n_slabs, norm_rows, 0)

    wgu = jnp.concatenate([wg_ref[...].astype(BF16), wu_ref[...].astype(BF16)], axis=1)
    gu = jnp.dot(xn_ref[...], wgu, preferred_element_type=F32)
    g = gu[:, :tf]
    u = gu[:, tf:]
    h = (g * jax.nn.sigmoid(g) * u * FFN_RES_WEIGHT).astype(BF16)
    acc_ref[...] += jnp.dot(h, wd_ref[...].astype(BF16), preferred_element_type=F32)

    @pl.when(f == last_f)
    def _():
        def finish_rows(r, carry):
            if epilogue == "final":
                y = acc_ref[slab(r), :]
                acc_ref[slab(r), :] = y * _rms_scale(y) * eg_ref[...]
            elif epilogue == "emit":
                y = acc_ref[slab(r), :]
                xn_ref[slab(r), :] = (y * _rms_scale(y) * eg_ref[...]).astype(BF16)
                n_copy(i, r).start()
            y_copy(i, r).start()
            return carry

        lax.fori_loop(0, n_slabs, finish_rows, 0)

        @pl.when(i == n_tiles - 1)
        def _():
            def drain_rows(r, carry):
                y_copy(i, r).wait()
                if epilogue == "emit":
                    n_copy(i, r).wait()
                return carry

            lax.fori_loop(0, n_slabs, drain_rows, 0)


def _ffn(x2d, norm_g, w_gate, w_up, w_down, epilogue_g, *, epilogue):
    m, d = x2d.shape
    dff = w_gate.shape[1]
    tm = _pick_tile(m, 1024, FFN_NORM_ROWS)
    tf = _pick_tile(dff, 256, LANES)
    assert tm % FFN_NORM_ROWS == 0 and epilogue in ("none", "final", "emit")
    hbm = pl.BlockSpec(memory_space=pl.ANY)
    y_shape = jax.ShapeDtypeStruct((m, d), F32)
    emit = epilogue == "emit"
    return pl.pallas_call(
        functools.partial(_ffn_kernel, tm=tm, tf=tf, epilogue=epilogue),
        out_shape=(y_shape, jax.ShapeDtypeStruct((m, d), BF16)) if emit else y_shape,
        grid=(m // tm, dff // tf),
        in_specs=[
            hbm,
            pl.BlockSpec((1, d), lambda i, f: (0, 0)),
            pl.BlockSpec((d, tf), lambda i, f: (0, f)),
            pl.BlockSpec((d, tf), lambda i, f: (0, f)),
            pl.BlockSpec((tf, d), lambda i, f: (f, 0)),
            pl.BlockSpec((1, d), lambda i, f: (0, 0)),
        ],
        out_specs=(hbm, hbm) if emit else hbm,
        scratch_shapes=[
            pltpu.VMEM((tm, d), F32),
            pltpu.VMEM((tm, d), BF16),
            pltpu.SemaphoreType.DMA((3, tm // FFN_NORM_ROWS)),
        ],
        compiler_params=pltpu.CompilerParams(
            dimension_semantics=("arbitrary", "arbitrary"), vmem_limit_bytes=V7X_VMEM_LIMIT_BYTES
        ),
        name="ffn_" + epilogue,
    )(x2d, norm_g.reshape(1, d), w_gate, w_up, w_down, epilogue_g.reshape(1, d))


INPROJ_SUB = 2 * LANES


def _inproj_kernel(xn_ref, wt_ref, wgt_ref, wo_ref, p_ref, gt_ref, wo_bf_ref):
    n = pl.program_id(1)
    trans_b = (((1,), (1,)), ((), ()))

    wo_bf_ref[...] = wo_ref[...].astype(BF16)

    @pl.when(n == 0)
    def _():
        wg = jnp.concatenate([wgt_ref[...], jnp.zeros_like(wgt_ref)], axis=0).astype(BF16)
        gt = lax.dot_general(wg, xn_ref[...], trans_b, preferred_element_type=F32)
        gt_ref[...] = gt[:SUBLANES, :]

    for j in range(wt_ref.shape[0] // INPROJ_SUB):
        cols = pl.ds(j * INPROJ_SUB, INPROJ_SUB)
        w = wt_ref[cols, :].astype(BF16)
        p_ref[:, cols] = lax.dot_general(xn_ref[...], w, trans_b, preferred_element_type=F32).astype(BF16)


def _in_proj(xn, w_in, w_out):
    m, d = xn.shape
    assert N_GATES == SUBLANES and PROJ_W % SUBLANES == 0
    tm = _pick_tile(m, 2048, LANES)
    tn = _pick_tile(PROJ_W, 512, INPROJ_SUB)
    n_cols = PROJ_W // tn
    n_steps = (m // tm) * n_cols
    wo_rows, wo_cols = w_out.shape
    cast_rows = next(r for r in range(2 * SUBLANES, wo_rows + 1, 2 * SUBLANES)
                     if wo_rows % r == 0 and wo_rows // r <= n_steps)
    n_cast = wo_rows // cast_rows

    def cast_block(i, n):
        return (jnp.minimum(i * n_cols + n, n_cast - 1), 0)

    w_t = w_in.T
    return pl.pallas_call(
        _inproj_kernel,
        out_shape=(
            jax.ShapeDtypeStruct((m, PROJ_W), BF16),
            jax.ShapeDtypeStruct((SUBLANES, m), F32),
            jax.ShapeDtypeStruct((wo_rows, wo_cols), BF16),
        ),
        grid=(m // tm, n_cols),
        in_specs=[
            pl.BlockSpec((tm, d), lambda i, n: (i, 0)),
            pl.BlockSpec((tn, d), lambda i, n: (n, 0)),
            pl.BlockSpec((N_GATES, d), lambda i, n: (PROJ_W // N_GATES, 0)),
            pl.BlockSpec((cast_rows, wo_cols), cast_block),
        ],
        out_specs=(
            pl.BlockSpec((tm, tn), lambda i, n: (i, n)),
            pl.BlockSpec((SUBLANES, tm), lambda i, n: (0, i)),
            pl.BlockSpec((cast_rows, wo_cols), cast_block),
        ),
        compiler_params=pltpu.CompilerParams(
            dimension_semantics=("arbitrary", "arbitrary"), vmem_limit_bytes=V7X_VMEM_LIMIT_BYTES
        ),
        name="in_proj",
    )(xn, w_t, w_t, w_out)


def _retention_body(lg_ref, q_ref, k_ref, v_ref, g_ref, cos_ref, sin_ref, hn_ref, o_ref, state_ref, *, n_chunks):
    row = lax.broadcasted_iota(jnp.int32, (CHUNK, CHUNK), 0)
    col = lax.broadcasted_iota(jnp.int32, (CHUNK, CHUNK), 1)
    rel = (row - col).astype(F32)
    pos = lax.broadcasted_iota(jnp.int32, (CHUNK, 1), 0).astype(F32)
    k_scale = RET_QK_DIM ** -0.5
    trans_b = (((1,), (1,)), ((), ()))

    decays = []
    for h in range(RET_HEADS):
        lg = lg_ref[h]
        decays.append((
            jnp.where(rel >= 0, jnp.exp(lg * rel), 0.0),
            jnp.exp(lg * (pos + 1.0)),
            jnp.exp(lg * (CHUNK - 1.0 - pos)),
            jnp.exp(lg * CHUNK),
        ))

    for ci in range(n_chunks):
        rows = pl.ds(ci * CHUNK, CHUNK)
        cos = cos_ref[rows, :]
        sin = sin_ref[rows, :]
        for h in range(RET_HEADS):
            intra, q_decay, k_decay, chunk_decay = decays[h]
            qk_cols = pl.ds(h * RET_QK_DIM, RET_QK_DIM)
            v_cols = pl.ds(h * RET_V_DIM, RET_V_DIM)
            q = q_ref[rows, qk_cols].astype(F32)
            k = k_ref[rows, qk_cols].astype(F32)
            q = q * cos + pltpu.roll(q, RET_QK_DIM // 2, 1) * sin
            k = (k * cos + pltpu.roll(k, RET_QK_DIM // 2, 1) * sin) * k_scale
            qb = q.astype(BF16)
            v = v_ref[rows, v_cols]
            state = state_ref[h]
            scores = lax.dot_general(qb, k.astype(BF16), trans_b, preferred_element_type=F32) * intra
            out = jnp.dot(scores.astype(BF16), v, preferred_element_type=F32)
            out = out + jnp.dot(qb, state.astype(BF16), preferred_element_type=F32) * q_decay
            kd_t = (k * k_decay).T.astype(BF16)
            state_ref[h] = state * chunk_decay + jnp.dot(kd_t, v, preferred_element_type=F32)
            gate = g_ref[rows, v_cols].astype(F32)
            y = out * _rms_scale(out) * hn_ref[:, v_cols] * (gate * jax.nn.sigmoid(gate))
            o_ref[rows, v_cols] = y.astype(BF16)


def _log_sigmoid(x):
    return jnp.minimum(x, 0.0) - jnp.log1p(jnp.exp(-jnp.abs(x)))


def _mlstm_body(ib_ref, fb_ref, q_ref, k_ref, v_ref, og_ref, gt_ref, cwq_ref, cwk_ref, cbq_ref, cbk_ref, hn_ref,
                o_ref, c_ref, m_ref, qprev_ref, kprev_ref, qs_ref, ks_ref, *, n_chunks, after_prepare):
    tr = n_chunks * CHUNK

    sel_row = lax.broadcasted_iota(jnp.int32, (CHUNK, 2 * CHUNK), 0)
    sel_col = lax.broadcasted_iota(jnp.int32, (CHUNK, 2 * CHUNK), 1)
    selectors = [(sel_col == sel_row + CHUNK - (CONV_WIDTH - 1 - j)).astype(BF16) for j in range(CONV_WIDTH - 1)]

    def conv_silu(prev_ref, src_ref, w_ref, b_ref, dst_ref, scale):
        prev = prev_ref[...]
        for ci in range(n_chunks):
            cur = src_ref[pl.ds(ci * CHUNK, CHUNK), :]
            both = jnp.concatenate([prev, cur], axis=0)
            acc = b_ref[...] + cur.astype(F32) * w_ref[pl.ds(CONV_WIDTH - 1, 1), :]
            for j in range(CONV_WIDTH - 1):
                shifted = jnp.dot(selectors[j], both, preferred_element_type=F32)
                acc = acc + shifted * w_ref[pl.ds(j, 1), :]
            dst_ref[pl.ds(ci * CHUNK, CHUNK), :] = (acc * jax.nn.sigmoid(acc) * scale).astype(BF16)
            prev = cur
        prev_ref[...] = prev

    conv_silu(qprev_ref, q_ref, cwq_ref, cbq_ref, qs_ref, 1.0)
    conv_silu(kprev_ref, k_ref, cwk_ref, cbk_ref, ks_ref, MLSTM_QK_DIM ** -0.5)

    row = lax.broadcasted_iota(jnp.int32, (CHUNK, CHUNK), 0)
    col = lax.broadcasted_iota(jnp.int32, (CHUNK, CHUNK), 1)
    causal = col <= row
    ones_col = (lax.broadcasted_iota(jnp.int32, (CHUNK, AUG_W), 1) == 0).astype(BF16)
    trans_b = (((1,), (1,)), ((), ()))

    sub = lax.broadcasted_iota(jnp.int32, (SUBLANES, tr), 0)
    pos = lax.broadcasted_iota(jnp.int32, (SUBLANES, tr), 1) % CHUNK
    bias = jnp.zeros((SUBLANES, tr), F32)
    for h in range(MLSTM_HEADS):
        bias = jnp.where(sub == h, ib_ref[h], bias)
        bias = jnp.where(sub == MLSTM_HEADS + h, fb_ref[h], bias)
    pre = gt_ref[...] + bias
    cum = _log_sigmoid(pre)
    shift = 1
    while shift < CHUNK:
        cum = cum + jnp.where(pos >= shift, pltpu.roll(cum, shift, 1), 0.0)
        shift *= 2
    gate_rows = jnp.where(sub < MLSTM_HEADS, pre, cum)
    pad_rows = jnp.zeros((CHUNK - SUBLANES, CHUNK), F32)

    after_prepare()

    for ci in range(n_chunks):
        rows = pl.ds(ci * CHUNK, CHUNK)
        g_rows = gate_rows[:, ci * CHUNK:(ci + 1) * CHUNK]
        g_cols = jnp.concatenate([g_rows, pad_rows], axis=0).T
        for h in range(MLSTM_HEADS):
            qk_cols = pl.ds(h * MLSTM_QK_DIM, MLSTM_QK_DIM)
            v_cols = pl.ds(h * MLSTM_V_DIM, MLSTM_V_DIM)
            i_row = g_rows[h:h + 1, :]
            b_row = g_rows[MLSTM_HEADS + h:MLSTM_HEADS + h + 1, :]
            i_col = g_cols[:, h:h + 1]
            b_col = g_cols[:, MLSTM_HEADS + h:MLSTM_HEADS + h + 1]

            m_prev = m_ref[h, :, 0:1]
            d_log = jnp.where(causal, b_col - b_row + i_row, -jnp.inf)
            inter = b_col + m_prev
            m_t = jnp.maximum(inter, jnp.max(d_log, axis=1, keepdims=True))
            w_inter = jnp.exp(inter - m_t)
            w_intra = jnp.exp(d_log - m_t)

            q = qs_ref[rows, qk_cols]
            k = ks_ref[rows, qk_cols]
            v_aug = jnp.concatenate([v_ref[rows, v_cols], ones_col], axis=1)
            state = c_ref[h]
            s = lax.dot_general(q, k, trans_b, preferred_element_type=F32) * w_intra
            tot = jnp.dot(s.astype(BF16), v_aug, preferred_element_type=F32)
            tot = tot + w_inter * jnp.dot(q, state.astype(BF16), preferred_element_type=F32)
            num = tot[:, :MLSTM_V_DIM]
            nq = tot[:, MLSTM_V_DIM:MLSTM_V_DIM + 1]
            hid = num / jnp.maximum(jnp.abs(nq), jnp.exp(-m_t))

            b_last = b_row[:, CHUNK - 1:CHUNK]
            m_new = m_t[CHUNK - 1:CHUNK, :]
            w_state = jnp.exp(b_last + m_prev - m_new)
            w_key = jnp.exp(b_last - b_col + i_col - m_new)
            kw_t = (k.astype(F32) * w_key).T.astype(BF16)
            c_ref[h] = w_state * state + jnp.dot(kw_t, v_aug, preferred_element_type=F32)
            m_ref[h] = jnp.broadcast_to(m_new, (1, LANES))

            og = og_ref[rows, v_cols].astype(F32)
            y = hid * _rms_scale(hid) * hn_ref[:, v_cols] * jax.nn.sigmoid(og)
            o_ref[rows, v_cols] = y.astype(BF16)


N_RET_IN = 8
N_MLSTM_IN = 12


def _mixer_kernel(*refs, n_chunks):
    ret_in = refs[:N_RET_IN]
    mlstm_in = refs[N_RET_IN:N_RET_IN + N_MLSTM_IN]
    ret_o, mh_o, state_ref, c_ref, m_ref, qprev_ref, kprev_ref, qs_ref, ks_ref = refs[N_RET_IN + N_MLSTM_IN:]

    @pl.when(pl.program_id(1) == 0)
    def _():
        state_ref[...] = jnp.zeros_like(state_ref)
        c_ref[...] = jnp.zeros_like(c_ref)
        m_ref[...] = jnp.zeros_like(m_ref)
        qprev_ref[...] = jnp.zeros_like(qprev_ref)
        kprev_ref[...] = jnp.zeros_like(kprev_ref)

    _mlstm_body(*mlstm_in, mh_o, c_ref, m_ref, qprev_ref, kprev_ref, qs_ref, ks_ref, n_chunks=n_chunks,
                after_prepare=functools.partial(_retention_body, *ret_in, ret_o, state_ref, n_chunks=n_chunks))


def _mixer_heads(proj, gates_t, log_gamma, cos2, sin2, ret_head_norm, conv_w, conv_b, igate_b, fgate_b,
                 mlstm_head_norm, batch, seq):
    m = proj.shape[0]
    tr = _pick_tile(seq, 512, CHUNK)
    tiles = seq // tr
    base = 2 * RET_QK_W + 2 * RET_V_W
    assert RET_V_W == 2 * RET_QK_W and base % MLSTM_V_W == 0 and MLSTM_V_W == 2 * MLSTM_QK_W
    q_blk = base // MLSTM_QK_W
    v_blk = (base + 2 * MLSTM_QK_W) // MLSTM_V_W
    smem = pl.BlockSpec(memory_space=pltpu.SMEM)
    return pl.pallas_call(
        functools.partial(_mixer_kernel, n_chunks=tr // CHUNK),
        out_shape=(jax.ShapeDtypeStruct((m, RET_V_W), BF16), jax.ShapeDtypeStruct((m, MLSTM_V_W), BF16)),
        grid=(batch, tiles),
        in_specs=[
            smem,
            pl.BlockSpec((tr, RET_QK_W), lambda b, c: (b * tiles + c, 0)),
            pl.BlockSpec((tr, RET_QK_W), lambda b, c: (b * tiles + c, 1)),
            pl.BlockSpec((tr, RET_V_W), lambda b, c: (b * tiles + c, 1)),
            pl.BlockSpec((tr, RET_V_W), lambda b, c: (b * tiles + c, 2)),
            pl.BlockSpec((tr, RET_QK_DIM), lambda b, c: (c, 0)),
            pl.BlockSpec((tr, RET_QK_DIM), lambda b, c: (c, 0)),
            pl.BlockSpec((1, RET_V_W), lambda b, c: (0, 0)),
            smem,
            smem,
            pl.BlockSpec((tr, MLSTM_QK_W), lambda b, c: (b * tiles + c, q_blk)),
            pl.BlockSpec((tr, MLSTM_QK_W), lambda b, c: (b * tiles + c, q_blk + 1)),
            pl.BlockSpec((tr, MLSTM_V_W), lambda b, c: (b * tiles + c, v_blk)),
            pl.BlockSpec((tr, MLSTM_V_W), lambda b, c: (b * tiles + c, v_blk + 1)),
            pl.BlockSpec((SUBLANES, tr), lambda b, c: (0, b * tiles + c)),
            pl.BlockSpec((CONV_WIDTH, MLSTM_QK_W), lambda b, c: (0, 0)),
            pl.BlockSpec((CONV_WIDTH, MLSTM_QK_W), lambda b, c: (0, 1)),
            pl.BlockSpec((1, MLSTM_QK_W), lambda b, c: (0, 0)),
            pl.BlockSpec((1, MLSTM_QK_W), lambda b, c: (0, 1)),
            pl.BlockSpec((1, MLSTM_V_W), lambda b, c: (0, 0)),
        ],
        out_specs=(
            pl.BlockSpec((tr, RET_V_W), lambda b, c: (b * tiles + c, 0)),
            pl.BlockSpec((tr, MLSTM_V_W), lambda b, c: (b * tiles + c, 0)),
        ),
        scratch_shapes=[
            pltpu.VMEM((RET_HEADS, RET_QK_DIM, RET_V_DIM), F32),
            pltpu.VMEM((MLSTM_HEADS, MLSTM_QK_DIM, MLSTM_V_DIM + AUG_W), F32),
            pltpu.VMEM((MLSTM_HEADS, 1, LANES), F32),
            pltpu.VMEM((CHUNK, MLSTM_QK_W), BF16),
            pltpu.VMEM((CHUNK, MLSTM_QK_W), BF16),
            pltpu.VMEM((tr, MLSTM_QK_W), BF16),
            pltpu.VMEM((tr, MLSTM_QK_W), BF16),
        ],
        compiler_params=pltpu.CompilerParams(
            dimension_semantics=("parallel", "arbitrary"), vmem_limit_bytes=V7X_VMEM_LIMIT_BYTES
        ),
        name="mixer_heads",
    )(log_gamma, proj, proj, proj, proj, cos2, sin2, ret_head_norm.reshape(1, RET_V_W),
      igate_b, fgate_b, proj, proj, proj, proj, gates_t, conv_w, conv_w, conv_b.reshape(1, -1),
      conv_b.reshape(1, -1), mlstm_head_norm.reshape(1, MLSTM_V_W))


def _outproj_kernel(x_ref, r_ref, h_ref, wr_ref, wh_ref, o_ref):
    acc = jnp.dot(r_ref[...], wr_ref[...], preferred_element_type=F32)
    acc = acc + jnp.dot(h_ref[...], wh_ref[...], preferred_element_type=F32)
    o_ref[...] = x_ref[...] + acc


def _out_proj(x2d, ret, mh, wo):
    m, d = x2d.shape
    tm = _pick_tile(m, 1024, SUBLANES)
    tn = _pick_tile(d, 1024, LANES)
    return pl.pallas_call(
        _outproj_kernel,
        out_shape=jax.ShapeDtypeStruct((m, d), F32),
        grid=(m // tm, d // tn),
        in_specs=[
            pl.BlockSpec((tm, tn), lambda i, n: (i, n)),
            pl.BlockSpec((tm, RET_V_W), lambda i, n: (i, 0)),
            pl.BlockSpec((tm, MLSTM_V_W), lambda i, n: (i, 0)),
            pl.BlockSpec((RET_V_W, tn), lambda i, n: (0, n)),
            pl.BlockSpec((MLSTM_V_W, tn), lambda i, n: (1, n)),
        ],
        out_specs=pl.BlockSpec((tm, tn), lambda i, n: (i, n)),
        compiler_params=pltpu.CompilerParams(
            dimension_semantics=("parallel", "parallel"), vmem_limit_bytes=V7X_VMEM_LIMIT_BYTES
        ),
        name="out_proj",
    )(x2d, ret, mh, wo, wo)


def _rotary_tables(seq):
    half = RET_QK_DIM // 2
    inv_freq = ROPE_BASE ** (-jnp.arange(half, dtype=F32) / half)
    ang = jnp.arange(seq, dtype=F32)[:, None] * inv_freq[None, :]
    cos, sin = jnp.cos(ang), jnp.sin(ang)
    return jnp.concatenate([cos, cos], axis=1), jnp.concatenate([-sin, sin], axis=1)


def kernel(x, ffn1_norm, ffn1_w_gate, ffn1_w_up, ffn1_w_down, mix_norm, w_in, conv_w, conv_b, igate_b, fgate_b,
           ret_head_norm, mlstm_head_norm, w_out, ffn2_norm, ffn2_w_gate, ffn2_w_up, ffn2_w_down, final_norm):
    batch, seq, d = x.shape
    depth = ffn1_norm.shape[0]
    assert seq % CHUNK == 0 and w_in.shape[-1] == PROJ_W + N_GATES and w_out.shape[1] == RET_V_W + MLSTM_V_W
    log_gamma = jnp.log1p(-jnp.exp2(-5.0 - jnp.arange(RET_HEADS, dtype=F32)))
    cos2, sin2 = _rotary_tables(seq)
    h = x.reshape(batch * seq, d)
    for layer in range(depth):
        last = layer == depth - 1
        h, xn = _ffn(h, ffn1_norm[layer], ffn1_w_gate[layer], ffn1_w_up[layer], ffn1_w_down[layer], mix_norm[layer],
                     epilogue="emit")
        proj, gates_t, wo = _in_proj(xn, w_in[layer], w_out[layer])
        ret, mh = _mixer_heads(proj, gates_t, log_gamma, cos2, sin2, ret_head_norm[layer], conv_w[layer],
                               conv_b[layer], igate_b[layer], fgate_b[layer], mlstm_head_norm[layer], batch, seq)
        h = _out_proj(h, ret, mh, wo)
        h = _ffn(h, ffn2_norm[layer], ffn2_w_gate[layer], ffn2_w_up[layer], ffn2_w_down[layer], final_norm,
                 epilogue="final" if last else "none")
    return h.reshape(batch, seq, d)
```
